```python
import jax, jax.numpy as jnp
from jax import lax
import numpy as np

D_MODEL = 2048
BATCH = 4
SEQ = 2048
DEPTH = 4
DEC_BATCH = 8
DEC_SEQ = 8
PAST_LEN = 16384
PAGE_SIZE = 128

N_A_LAYERS = DEPTH // 2
N_B_LAYERS = DEPTH - N_A_LAYERS
D_FF = 4 * D_MODEL
CHUNK = 128
SGU_WIDTH = D_MODEL
SGU_GROUPS = 8
SGU_GROUP_DIM = SGU_WIDTH // SGU_GROUPS
N_HEADS = 16
HEAD_DIM = D_MODEL // N_HEADS
QUERY_BLOCK = 128
RMS_EPS = 1e-6
SB_BIAS_INIT = -6.0

kernel_name = "yoco_gmlp_stickbreaking_decoder_step"


def rmsnorm(x, g):
    xf = x.astype(jnp.float32)
    y = xf * lax.rsqrt(jnp.mean(xf * xf, axis=-1, keepdims=True) + RMS_EPS)
    return (y * g.astype(jnp.float32)).astype(x.dtype)


def sq_relu_mlp(x, w_up, w_down):
    h = jax.nn.relu(x @ w_up)
    return (h * h) @ w_down


def spatial_gating(z, w_s, b_s):
    B, T, E = z.shape
    L = min(T, CHUNK)
    n_chunks = T // L
    zc = z.reshape(B, n_chunks, L, SGU_GROUPS, SGU_GROUP_DIM)
    w = jnp.tril(w_s[:, :L, :L])
    s = jnp.einsum('gij,bnjgc->bnigc', w, zc) + b_s[:, :L].T[None, None, :, :, None]
    return s.reshape(B, T, E)


def chunk_mlp_mixer(h, w_in, g_v, w_s, b_s, w_out):
    u, v = jnp.split(h @ w_in, 2, axis=-1)
    z = rmsnorm(v, g_v)
    s = spatial_gating(z, w_s, b_s)
    return (u * s) @ w_out, z


def _stick_breaking_block(q, k, v, bias, q_pos, k_pos):
    z = jnp.einsum('bqhd,bkhd->bhqk', q, k).astype(jnp.float32) * (HEAD_DIM ** -0.5)
    z = z + bias.astype(jnp.float32)[None, :, None, None]
    mask = k_pos[None, :] < q_pos[:, None]
    log_beta = jax.nn.log_sigmoid(z)
    log_rem = jnp.where(mask, log_beta - z, 0.0)
    log_surv = lax.cumsum(log_rem, axis=3, reverse=True) - log_rem
    a = jnp.where(mask, jnp.exp(log_beta + log_surv), 0.0)
    return jnp.einsum('bhqk,bkhd->bqhd', a.astype(v.dtype), v)


def stick_breaking_attention(q, k, v, bias, q_pos, k_pos):
    B, Tq, H, Dh = q.shape
    if Tq <= QUERY_BLOCK:
        return _stick_breaking_block(q, k, v, bias, q_pos, k_pos)
    nb = Tq // QUERY_BLOCK
    qb = jnp.moveaxis(q.reshape(B, nb, QUERY_BLOCK, H, Dh), 1, 0)
    pb = q_pos.reshape(nb, QUERY_BLOCK)
    out = lax.map(lambda a: _stick_breaking_block(a[0], k, v, bias, a[1], k_pos), (qb, pb))
    return jnp.moveaxis(out, 0, 1).reshape(B, Tq, H, Dh)


def trunk(x, past_k, past_v, norm_mix, norm_ffn, w_ffn_up, w_ffn_down,
          a_w_in, a_norm_v, a_w_spatial, a_b_spatial, a_w_out,
          kv_norm, w_kv, b_w_q, b_logit_bias, b_w_out, norm_final):
    B, T, _ = x.shape
    past = 0 if past_k is None else past_k.shape[1]
    a_states = []
    for layer in range(DEPTH):
        h = rmsnorm(x, norm_mix[layer])
        if layer < N_A_LAYERS:
            i = layer
            mix, z = chunk_mlp_mixer(h, a_w_in[i], a_norm_v[i], a_w_spatial[i],
                                     a_b_spatial[i], a_w_out[i])
            a_states.append(z)
        else:
            if layer == N_A_LAYERS:
                k_h, v_h = jnp.split(rmsnorm(x, kv_norm) @ w_kv, 2, axis=-1)
                k_new = k_h.reshape(B, T, N_HEADS, HEAD_DIM)
                v_new = v_h.reshape(B, T, N_HEADS, HEAD_DIM)
                if past_k is None:
                    k_all, v_all = k_new, v_new
                else:
                    k_all = jnp.concatenate([past_k, k_new], axis=1)
                    v_all = jnp.concatenate([past_v, v_new], axis=1)
                q_pos = past + jnp.arange(T, dtype=jnp.int32)
                k_pos = jnp.arange(past + T, dtype=jnp.int32)
            j = layer - N_A_LAYERS
            q = (h @ b_w_q[j]).reshape(B, T, N_HEADS, HEAD_DIM)
            o = stick_breaking_attention(q, k_all, v_all, b_logit_bias[j], q_pos, k_pos)
            mix = o.reshape(B, T, N_HEADS * HEAD_DIM) @ b_w_out[j]
        x = x + mix
        x = x + sq_relu_mlp(rmsnorm(x, norm_ffn[layer]), w_ffn_up[layer], w_ffn_down[layer])
    return rmsnorm(x, norm_final), k_new, v_new, jnp.stack(a_states)


def setup_inputs(seed: int = 0) -> dict:
    key = jax.random.key(seed)
    ks = jax.random.split(key, 24)
    f32 = jnp.float32

    def w(k, shape, fan_in):
        return jax.random.normal(k, shape, f32) * (fan_in ** -0.5)

    def gain(k, shape):
        return 1.0 + 0.02 * jax.random.normal(k, shape, f32)

    n_pages = PAST_LEN // PAGE_SIZE
    n_used = DEC_BATCH * n_pages
    n_pool = n_used + max(1, n_used // 4)
    page_table = jax.random.permutation(ks[4], n_pool)[:n_used].reshape(DEC_BATCH, n_pages).astype(jnp.int32)
    hd = N_HEADS * HEAD_DIM
    return {
        "x_prompt": jax.random.normal(ks[0], (BATCH, SEQ, D_MODEL), f32),
        "x_sample": jax.random.normal(ks[1], (DEC_BATCH, DEC_SEQ, D_MODEL), f32),
        "cache_k": jax.random.normal(ks[2], (n_pool, PAGE_SIZE, N_HEADS, HEAD_DIM), f32),
        "cache_v": jax.random.normal(ks[3], (n_pool, PAGE_SIZE, N_HEADS, HEAD_DIM), f32),
        "page_table": page_table,
        "norm_mix": gain(ks[5], (DEPTH, D_MODEL)),
        "norm_ffn": gain(ks[6], (DEPTH, D_MODEL)),
        "w_ffn_up": w(ks[7], (DEPTH, D_MODEL, D_FF), D_MODEL),
        "w_ffn_down": w(ks[8], (DEPTH, D_FF, D_MODEL), D_FF),
        "a_w_in": w(ks[9], (N_A_LAYERS, D_MODEL, 2 * SGU_WIDTH), D_MODEL),
        "a_norm_v": gain(ks[10], (N_A_LAYERS, SGU_WIDTH)),
        "a_w_spatial": w(ks[11], (N_A_LAYERS, SGU_GROUPS, CHUNK, CHUNK), CHUNK),
        "a_b_spatial": gain(ks[12], (N_A_LAYERS, SGU_GROUPS, CHUNK)),
        "a_w_out": w(ks[13], (N_A_LAYERS, SGU_WIDTH, D_MODEL), SGU_WIDTH),
        "kv_norm": gain(ks[14], (D_MODEL,)),
        "w_kv": w(ks[15], (D_MODEL, 2 * hd), D_MODEL),
        "b_w_q": w(ks[16], (N_B_LAYERS, D_MODEL, hd), D_MODEL),
        "b_logit_bias": SB_BIAS_INIT + 0.1 * jax.random.normal(ks[19], (N_B_LAYERS, N_HEADS), f32),
        "b_w_out": w(ks[17], (N_B_LAYERS, hd, D_MODEL), hd),
        "norm_final": gain(ks[18], (D_MODEL,)),
    }


def reference(x_prompt, x_sample, cache_k, cache_v, page_table,
              norm_mix, norm_ffn, w_ffn_up, w_ffn_down,
              a_w_in, a_norm_v, a_w_spatial, a_b_spatial, a_w_out,
              kv_norm, w_kv, b_w_q, b_logit_bias, b_w_out, norm_final):
    y_prompt, k_prompt, v_prompt, _ = trunk(
        x_prompt, None, None, norm_mix, norm_ffn, w_ffn_up, w_ffn_down,
        a_w_in, a_norm_v, a_w_spatial, a_b_spatial, a_w_out,
        kv_norm, w_kv, b_w_q, b_logit_bias, b_w_out, norm_final)
    db = x_sample.shape[0]
    past_len = page_table.shape[1] * cache_k.shape[1]
    past_k = cache_k[page_table].reshape(db, past_len, N_HEADS, HEAD_DIM)
    past_v = cache_v[page_table].reshape(db, past_len, N_HEADS, HEAD_DIM)
    y_sample, k_sample, v_sample, sgu_v_sample = trunk(
        x_sample, past_k, past_v, norm_mix, norm_ffn, w_ffn_up, w_ffn_down,
        a_w_in, a_norm_v, a_w_spatial, a_b_spatial, a_w_out,
        kv_norm, w_kv, b_w_q, b_logit_bias, b_w_out, norm_final)
    return (y_prompt, y_sample, k_prompt, v_prompt, k_sample, v_sample, sgu_v_sample)
```

```python
import functools

import jax
import jax.numpy as jnp
from jax import lax
from jax.experimental import pallas as pl
from jax.experimental.pallas import tpu as pltpu

N_HEADS = 16
HEAD_DIM = 128
CHUNK = 128
SGU_GROUPS = 8
RMS_EPS = 1e-6

VMEM_LIMIT_BYTES = 56 * 1024 * 1024

F32 = jnp.float32
BF16 = jnp.bfloat16


def _params(*semantics):
    return pltpu.CompilerParams(dimension_semantics=semantics,
                                vmem_limit_bytes=VMEM_LIMIT_BYTES)


def _rmsnorm_kernel(x_ref, g_ref, o_ref):
    x = x_ref[...]
    ms = jnp.mean(x * x, axis=-1, keepdims=True)
    o_ref[...] = (x * lax.rsqrt(ms + RMS_EPS) * g_ref[...]).astype(o_ref.dtype)


def rmsnorm(x, g, out_dtype):
    m, d = x.shape
    tm = min(m, 512)
    assert m % tm == 0
    return pl.pallas_call(
        _rmsnorm_kernel,
        grid=(m // tm,),
        in_specs=[pl.BlockSpec((tm, d), lambda i: (i, 0)),
                  pl.BlockSpec((1, d), lambda i: (0, 0))],
        out_specs=pl.BlockSpec((tm, d), lambda i: (i, 0)),
        out_shape=jax.ShapeDtypeStruct((m, d), out_dtype),
        compiler_params=_params("parallel"),
        name="rmsnorm",
    )(x, g.reshape(1, d))


def _mm_kernel(*refs, nk, epilogue):
    if epilogue == "residual":
        a_ref, w_ref, r_ref, o_ref = refs[:4]
        scratch = refs[4:]
    else:
        a_ref, w_ref, o_ref = refs[:3]
        r_ref = None
        scratch = refs[3:]

    def finish(acc):
        if epilogue == "relu2":
            h = jnp.maximum(acc, 0.0)
            acc = h * h
        elif epilogue == "residual":
            acc = r_ref[...] + acc
        o_ref[...] = acc.astype(o_ref.dtype)

    part = jnp.dot(a_ref[...], w_ref[...], preferred_element_type=F32)
    if nk == 1:
        finish(part)
    else:
        acc_ref = scratch[0]
        k = pl.program_id(2)

        @pl.when(k == 0)
        def _():
            acc_ref[...] = part

        @pl.when(jnp.logical_and(k > 0, k < nk - 1))
        def _():
            acc_ref[...] += part

        @pl.when(k == nk - 1)
        def _():
            finish(acc_ref[...] + part)


def matmul(a, w, layer, epilogue, res=None):
    m, kdim = a.shape
    n = w.shape[2]
    tm = min(m, 1024)
    tn = min(n, 1024)
    tk = min(kdim, 2048)
    assert m % tm == 0 and n % tn == 0 and kdim % tk == 0
    nk = kdim // tk
    out_dtype = BF16 if epilogue in ("bf16", "relu2") else F32
    in_specs = [pl.BlockSpec((tm, tk), lambda i, j, k: (i, k)),
                pl.BlockSpec((None, tk, tn), lambda i, j, k: (layer, k, j))]
    args = [a, w]
    if epilogue == "residual":
        in_specs.append(pl.BlockSpec((tm, tn), lambda i, j, k: (i, j)))
        args.append(res)
    scratch = [pltpu.VMEM((tm, tn), F32)] if nk > 1 else []
    return pl.pallas_call(
        functools.partial(_mm_kernel, nk=nk, epilogue=epilogue),
        grid=(m // tm, n // tn, nk),
        in_specs=in_specs,
        out_specs=pl.BlockSpec((tm, tn), lambda i, j, k: (i, j)),
        out_shape=jax.ShapeDtypeStruct((m, n), out_dtype),
        scratch_shapes=scratch,
        compiler_params=_params("parallel", "parallel", "arbitrary"),
        name="matmul_" + epilogue,
    )(*args)


def _gate_kernel(u_ref, v_ref, g_ref, w_ref, b_ref, p_ref, *z_refs, L, sub, nchunk):
    v = v_ref[...]
    ms = jnp.mean(v * v, axis=-1, keepdims=True)
    z = v * lax.rsqrt(ms + RMS_EPS) * g_ref[...]
    if z_refs:
        z_refs[0][...] = z
    zb = z.astype(BF16)
    gdim = v.shape[1] // SGU_GROUPS
    row = lax.broadcasted_iota(jnp.int32, (L, L), 0)
    col = lax.broadcasted_iota(jnp.int32, (L, L), 1)
    mask = jnp.logical_and(row // sub == col // sub, col % sub <= row % sub)
    for g in range(SGU_GROUPS):
        wg = jnp.where(mask, w_ref[g], 0.0).astype(BF16)
        cs = slice(g * gdim, (g + 1) * gdim)
        for c in range(nchunk):
            rs = slice(c * L, (c + 1) * L)
            s = jnp.dot(wg, zb[rs, cs], preferred_element_type=F32) + b_ref[:, cs]
            p_ref[rs, cs] = (u_ref[rs, cs] * s).astype(BF16)


def spatial_gate(uv, g_v, w_tile, b_full, *, sub, emit_z):
    m, e2 = uv.shape
    e = e2 // 2
    L = w_tile.shape[1]
    rows = min(m, 512)
    assert m % rows == 0 and rows % L == 0
    nchunk = rows // L
    out_shape = [jax.ShapeDtypeStruct((m, e), BF16)]
    out_specs = [pl.BlockSpec((rows, e), lambda i: (i, 0))]
    if emit_z:
        out_shape.append(jax.ShapeDtypeStruct((m, e), F32))
        out_specs.append(pl.BlockSpec((rows, e), lambda i: (i, 0)))
    outs = pl.pallas_call(
        functools.partial(_gate_kernel, L=L, sub=sub, nchunk=nchunk),
        grid=(m // rows,),
        in_specs=[pl.BlockSpec((rows, e), lambda i: (i, 0)),
                  pl.BlockSpec((rows, e), lambda i: (i, 1)),
                  pl.BlockSpec((1, e), lambda i: (0, 0)),
                  pl.BlockSpec(w_tile.shape, lambda i: (0, 0, 0)),
                  pl.BlockSpec((L, e), lambda i: (0, 0))],
        out_specs=out_specs,
        out_shape=out_shape,
        compiler_params=_params("parallel"),
        name="spatial_gate",
    )(uv, uv, g_v.reshape(1, e), w_tile, b_full)
    return (outs[0], outs[1]) if emit_z else (outs[0], None)


def _sb_core(z, mask, c, uu):
    lr = -(jnp.maximum(z, 0.0) + jnp.log1p(jnp.exp(-jnp.abs(z))))
    if mask is not None:
        lr = jnp.where(mask, lr, 0.0)
    hi = lr.astype(BF16)
    lo = (lr - hi.astype(F32)).astype(BF16)
    incl = jnp.dot(jnp.concatenate([hi, lo], axis=1), uu, preferred_element_type=F32)
    a = jnp.exp(z + incl + c)
    if mask is not None:
        a = jnp.where(mask, a, 0.0)
    return a, c + incl[:, 0:1]


def _suffix_ones(tk):
    j = lax.broadcasted_iota(jnp.int32, (2 * tk, tk), 0) % tk
    s = lax.broadcasted_iota(jnp.int32, (2 * tk, tk), 1)
    return (j >= s).astype(BF16)


_NT = (((1,), (1,)), ((), ()))


def _attn_prompt_kernel(bias_ref, q_ref, k_ref, v_ref, o_ref, kb_ref, vb_ref, uu_ref, *, blk):
    h = pl.program_id(1)
    qi = pl.program_id(2)

    @pl.when(qi == 0)
    def _():
        kb_ref[...] = k_ref[...].astype(BF16)
        vb_ref[...] = v_ref[...].astype(BF16)
        uu_ref[...] = _suffix_ones(blk)

    q = q_ref[...]
    bias = bias_ref[h]
    scale = HEAD_DIM ** -0.5
    row = lax.broadcasted_iota(jnp.int32, (blk, blk), 0)
    col = lax.broadcasted_iota(jnp.int32, (blk, blk), 1)

    def block(kj, masked, c, acc):
        start = pl.multiple_of(kj * blk, blk)
        kblk = kb_ref[pl.ds(start, blk), :]
        vblk = vb_ref[pl.ds(start, blk), :]
        z = lax.dot_general(q, kblk, _NT, preferred_element_type=F32) * scale + bias
        a, c = _sb_core(z, (col < row) if masked else None, c, uu_ref[...])
        acc = acc + jnp.dot(a.astype(BF16), vblk, preferred_element_type=F32)
        return c, acc

    c, acc = block(qi, True, jnp.zeros((blk, 1), F32), jnp.zeros((blk, HEAD_DIM), F32))
    c, acc = lax.fori_loop(0, qi, lambda it, ca: block(qi - 1 - it, False, *ca), (c, acc))
    o_ref[...] = acc.astype(o_ref.dtype)


def attention_prompt(q, kv, bias, *, blk=256):
    b, t, _ = q.shape
    blk = min(blk, t)
    assert t % blk == 0
    return pl.pallas_call(
        functools.partial(_attn_prompt_kernel, blk=blk),
        grid=(b, N_HEADS, t // blk),
        in_specs=[pl.BlockSpec(memory_space=pltpu.SMEM),
                  pl.BlockSpec((None, blk, HEAD_DIM), lambda bi, h, qi: (bi, qi, h)),
                  pl.BlockSpec((None, t, HEAD_DIM), lambda bi, h, qi: (bi, 0, h)),
                  pl.BlockSpec((None, t, HEAD_DIM), lambda bi, h, qi: (bi, 0, N_HEADS + h))],
        out_specs=pl.BlockSpec((None, blk, HEAD_DIM), lambda bi, h, qi: (bi, qi, h)),
        out_shape=jax.ShapeDtypeStruct(q.shape, BF16),
        scratch_shapes=[pltpu.VMEM((t, HEAD_DIM), BF16),
                        pltpu.VMEM((t, HEAD_DIM), BF16),
                        pltpu.VMEM((2 * blk, blk), BF16)],
        compiler_params=_params("parallel", "parallel", "arbitrary"),
        name="attention_prompt",
    )(bias, q, kv, kv)


def _attn_sample_kernel(pt_ref, q_ref, kn_ref, vn_ref, kc_ref, vc_ref, bias_ref, o_ref,
                        qbd_ref, acc_ref, c_ref, uu_ref, kb_ref, vb_ref, *, n_pages, tq, page):
    del pt_ref
    j = pl.program_id(1)
    rows = N_HEADS * tq
    d = N_HEADS * HEAD_DIM
    scale = HEAD_DIM ** -0.5

    @pl.when(j == 0)
    def _():
        qt = jnp.concatenate([q_ref[...]] * N_HEADS, axis=0)
        rh = lax.broadcasted_iota(jnp.int32, (rows, d), 0) // tq
        ch = lax.broadcasted_iota(jnp.int32, (rows, d), 1) // HEAD_DIM
        qbd_ref[...] = jnp.where(rh == ch, qt, 0.0).astype(BF16)
        acc_ref[...] = jnp.zeros_like(acc_ref)
        c_ref[...] = jnp.zeros_like(c_ref)
        uu_ref[...] = _suffix_ones(page)

    def step(k, v, masked):
        z = lax.dot_general(qbd_ref[...], k, _NT, preferred_element_type=F32)
        z = z * scale + bias_ref[...]
        mask = None
        if masked:
            t_idx = lax.broadcasted_iota(jnp.int32, (rows, page), 0) % tq
            s_idx = lax.broadcasted_iota(jnp.int32, (rows, page), 1)
            mask = s_idx < t_idx
        a, c = _sb_core(z, mask, c_ref[...], uu_ref[...])
        c_ref[...] = c
        acc_ref[...] += jnp.dot(a.astype(BF16), v, preferred_element_type=F32)

    @pl.when(j == 0)
    def _():
        step(kn_ref[...].astype(BF16), vn_ref[...].astype(BF16), True)

    @pl.when(j > 0)
    def _():
        for h in range(N_HEADS):
            cs = slice(h * HEAD_DIM, (h + 1) * HEAD_DIM)
            kb_ref[:, cs] = kc_ref[pl.ds(h, page, stride=N_HEADS), :].astype(BF16)
            vb_ref[:, cs] = vc_ref[pl.ds(h, page, stride=N_HEADS), :].astype(BF16)
        step(kb_ref[...], vb_ref[...], False)

    @pl.when(j == n_pages)
    def _():
        for h in range(N_HEADS):
            o_ref[:, h * HEAD_DIM:(h + 1) * HEAD_DIM] = (
                acc_ref[h * tq:(h + 1) * tq, h * HEAD_DIM:(h + 1) * HEAD_DIM])


def attention_sample(q, k_new, v_new, cache_k, cache_v, page_table, bias):
    db, tq, d = q.shape
    page = cache_k.shape[1] // N_HEADS
    n_pages = page_table.shape[1]
    rows = N_HEADS * tq
    pad = ((0, 0), (0, page - tq), (0, 0))
    kn = jnp.pad(k_new, pad)
    vn = jnp.pad(v_new, pad)
    bias_col = jnp.repeat(bias, tq).reshape(rows, 1)

    def seq_block(bi, j, pt):
        return (bi, 0, 0)

    def page_block(bi, j, pt):
        return (pt[bi, n_pages - jnp.maximum(j, 1)], 0, 0)

    grid_spec = pltpu.PrefetchScalarGridSpec(
        num_scalar_prefetch=1,
        grid=(db, n_pages + 1),
        in_specs=[pl.BlockSpec((None, tq, d), seq_block),
                  pl.BlockSpec((None, page, d), seq_block),
                  pl.BlockSpec((None, page, d), seq_block),
                  pl.BlockSpec((None, page * N_HEADS, HEAD_DIM), page_block),
                  pl.BlockSpec((None, page * N_HEADS, HEAD_DIM), page_block),
                  pl.BlockSpec((rows, 1), lambda bi, j, pt: (0, 0))],
        out_specs=pl.BlockSpec((None, tq, d), seq_block),
        scratch_shapes=[pltpu.VMEM((rows, d), BF16),
                        pltpu.VMEM((rows, d), F32),
                        pltpu.VMEM((rows, 1), F32),
                        pltpu.VMEM((2 * page, page), BF16),
                        pltpu.VMEM((page, d), BF16),
                        pltpu.VMEM((page, d), BF16)],
    )
    return pl.pallas_call(
        functools.partial(_attn_sample_kernel, n_pages=n_pages, tq=tq, page=page),
        grid_spec=grid_spec,
        out_shape=jax.ShapeDtypeStruct((db, tq, d), F32),
        compiler_params=_params("parallel", "arbitrary"),
        name="attention_sample",
    )(page_table, q, kn, vn, cache_k, cache_v, bias_col)


def _trunk(x3, past, wts):
    b, t, d = x3.shape
    m = b * t
    x = x3.reshape(m, d)
    n_a = wts["a_w_in"].shape[0]
    depth = wts["w_ffn_up"].shape[0]

    sub = min(t, CHUNK)
    L = sub if sub == CHUNK else min(m, 64)
    reps = L // sub

    def ffn(x, layer):
        h = rmsnorm(x, wts["norm_ffn"][layer], BF16)
        h = matmul(h, wts["w_ffn_up"], layer, "relu2")
        return matmul(h, wts["w_ffn_down"], layer, "residual", res=x)

    z_states = []
    for layer in range(n_a):
        h = rmsnorm(x, wts["norm_mix"][layer], BF16)
        uv = matmul(h, wts["a_w_in"], layer, "f32")
        w_tile = jnp.tile(wts["a_w_spatial"][layer][:, :sub, :sub], (1, reps, reps))
        b_rows = jnp.tile(wts["a_b_spatial"][layer][:, :sub].T, (reps, 1))
        b_full = jnp.repeat(b_rows, d // SGU_GROUPS, axis=1)
        p, z = spatial_gate(uv, wts["a_norm_v"][layer], w_tile, b_full,
                            sub=sub, emit_z=past is not None)
        z_states.append(z)
        x = matmul(p, wts["a_w_out"], layer, "residual", res=x)
        x = ffn(x, layer)

    kv = matmul(rmsnorm(x, wts["kv_norm"], BF16), wts["w_kv"][None], 0, "f32")
    hd = N_HEADS * HEAD_DIM
    k_new = kv[:, :hd]
    v_new = kv[:, hd:]

    for layer in range(n_a, depth):
        j = layer - n_a
        h = rmsnorm(x, wts["norm_mix"][layer], BF16)
        if past is None:
            q = matmul(h, wts["b_w_q"], j, "bf16")
            o = attention_prompt(q.reshape(b, t, hd), kv.reshape(b, t, 2 * hd),
                                 wts["b_logit_bias"][j])
            o = o.reshape(m, hd)
        else:
            cache_k, cache_v, page_table = past
            q = matmul(h, wts["b_w_q"], j, "f32")
            o = attention_sample(q.reshape(b, t, hd), k_new.reshape(b, t, hd),
                                 v_new.reshape(b, t, hd), cache_k, cache_v, page_table,
                                 wts["b_logit_bias"][j])
            o = o.reshape(m, hd).astype(BF16)
        x = matmul(o, wts["b_w_out"], j, "residual", res=x)
        x = ffn(x, layer)

    y = rmsnorm(x, wts["norm_final"], F32).reshape(b, t, d)
    k_out = k_new.reshape(b, t, N_HEADS, HEAD_DIM)
    v_out = v_new.reshape(b, t, N_HEADS, HEAD_DIM)
    return y, k_out, v_out, z_states


def kernel(x_prompt, x_sample, cache_k, cache_v, page_table, norm_mix, norm_ffn,
           w_ffn_up, w_ffn_down, a_w_in, a_norm_v, a_w_spatial, a_b_spatial, a_w_out,
           kv_norm, w_kv, b_w_q, b_logit_bias, b_w_out, norm_final):
    wts = dict(
        norm_mix=norm_mix, norm_ffn=norm_ffn,
        w_ffn_up=w_ffn_up.astype(BF16), w_ffn_down=w_ffn_down.astype(BF16),
        a_w_in=a_w_in.astype(BF16), a_norm_v=a_norm_v, a_w_spatial=a_w_spatial,
        a_b_spatial=a_b_spatial, a_w_out=a_w_out.astype(BF16), kv_norm=kv_norm,
        w_kv=w_kv.astype(BF16), b_w_q=b_w_q.astype(BF16), b_logit_bias=b_logit_bias,
        b_w_out=b_w_out.astype(BF16), norm_final=norm_final)

    y_prompt, k_prompt, v_prompt, _ = _trunk(x_prompt, None, wts)

    n_pool, page, heads, dh = cache_k.shape
    past = (cache_k.reshape(n_pool, page * heads, dh), cache_v.reshape(n_pool, page * heads, dh),
            page_table)
    y_sample, k_sample, v_sample, z_states = _trunk(x_sample, past, wts)
    db, ts, d = x_sample.shape
    sgu_v_sample = jnp.stack(z_states).reshape(len(z_states), db, ts, d)
    return (y_prompt, y_sample, k_prompt, v_prompt, k_sample, v_sample, sgu_v_sample)
```

```python
import functools

import jax
import jax.numpy as jnp
from jax import lax
from jax.experimental import pallas as pl
from jax.experimental.pallas import tpu as pltpu

N_HEADS = 16
HEAD_DIM = 128
CHUNK = 128
SGU_GROUPS = 8
RMS_EPS = 1e-6

VMEM_LIMIT_BYTES = 56 * 1024 * 1024

F32 = jnp.float32
BF16 = jnp.bfloat16


def _params(*semantics):
    return pltpu.CompilerParams(dimension_semantics=semantics,
                                vmem_limit_bytes=VMEM_LIMIT_BYTES)


def _rmsnorm_kernel(x_ref, g_ref, o_ref):
    x = x_ref[...]
    ms = jnp.mean(x * x, axis=-1, keepdims=True)
    o_ref[...] = (x * lax.rsqrt(ms + RMS_EPS) * g_ref[...]).astype(o_ref.dtype)


def rmsnorm(x, g, out_dtype):
    m, d = x.shape
    tm = min(m, 512)
    assert m % tm == 0
    return pl.pallas_call(
        _rmsnorm_kernel,
        grid=(m // tm,),
        in_specs=[pl.BlockSpec((tm, d), lambda i: (i, 0)),
                  pl.BlockSpec((1, d), lambda i: (0, 0))],
        out_specs=pl.BlockSpec((tm, d), lambda i: (i, 0)),
        out_shape=jax.ShapeDtypeStruct((m, d), out_dtype),
        compiler_params=_params("parallel"),
        name="rmsnorm",
    )(x, g.reshape(1, d))


def _mm_ws_kernel(a_ref, w_ref, o_ref, wb_ref, *, epilogue):
    @pl.when(pl.program_id(1) == 0)
    def _():
        wb_ref[...] = w_ref[...].astype(BF16)

    acc = jnp.dot(a_ref[...], wb_ref[...], preferred_element_type=F32)
    if epilogue == "relu2":
        h = jnp.maximum(acc, 0.0)
        acc = h * h
    o_ref[...] = acc.astype(o_ref.dtype)


def matmul_ws(a, w, layer, epilogue, *, col0=0, n=None):
    m, kdim = a.shape
    n = w.shape[2] if n is None else n
    tm = min(m, 1024)
    tn = min(n, 1024)
    assert m % tm == 0 and n % tn == 0 and col0 % tn == 0 and w.shape[1] == kdim
    jb = col0 // tn
    out_dtype = F32 if epilogue == "f32" else BF16
    return pl.pallas_call(
        functools.partial(_mm_ws_kernel, epilogue=epilogue),
        grid=(n // tn, m // tm),
        in_specs=[pl.BlockSpec((tm, kdim), lambda j, i: (i, 0)),
                  pl.BlockSpec((None, kdim, tn), lambda j, i: (layer, 0, jb + j))],
        out_specs=pl.BlockSpec((tm, tn), lambda j, i: (i, j)),
        out_shape=jax.ShapeDtypeStruct((m, n), out_dtype),
        scratch_shapes=[pltpu.VMEM((kdim, tn), BF16)],
        compiler_params=_params("parallel", "arbitrary"),
        name="matmul_ws_" + epilogue,
    )(a, w)


def _mm_res_kernel(*refs, nk, emit_x, emit_norm):
    a_ref, w_ref, r_ref = refs[:3]
    pos = 3
    g_ref = None
    if emit_norm:
        g_ref = refs[pos]
        pos += 1
    x_ref = hn_ref = None
    if emit_x:
        x_ref = refs[pos]
        pos += 1
    if emit_norm:
        hn_ref = refs[pos]
        pos += 1
    scratch = refs[pos:]

    def finish(acc):
        x = r_ref[...] + acc
        if emit_x:
            x_ref[...] = x
        if emit_norm:
            ms = jnp.mean(x * x, axis=-1, keepdims=True)
            hn_ref[...] = (x * lax.rsqrt(ms + RMS_EPS) * g_ref[...]).astype(hn_ref.dtype)

    part = jnp.dot(a_ref[...], w_ref[...], preferred_element_type=F32)
    if nk == 1:
        finish(part)
    else:
        acc_ref = scratch[0]
        k = pl.program_id(1)

        @pl.when(k == 0)
        def _():
            acc_ref[...] = part

        @pl.when(jnp.logical_and(k > 0, k < nk - 1))
        def _():
            acc_ref[...] += part

        @pl.when(k == nk - 1)
        def _():
            finish(acc_ref[...] + part)


def matmul_res(a, w, layer, res, *, gain=None, norm_dtype=BF16, emit_x=True):
    m, kdim = a.shape
    d = w.shape[2]
    tm = min(m, 512)
    tk = min(kdim, 2048)
    assert m % tm == 0 and kdim % tk == 0 and res.shape == (m, d)
    nk = kdim // tk
    emit_norm = gain is not None
    assert emit_x or emit_norm
    row_spec = pl.BlockSpec((tm, d), lambda i, k: (i, 0))
    in_specs = [pl.BlockSpec((tm, tk), lambda i, k: (i, k)),
                pl.BlockSpec((None, tk, d), lambda i, k: (layer, k, 0)),
                row_spec]
    args = [a, w, res]
    out_shape, out_specs = [], []
    if emit_norm:
        in_specs.append(pl.BlockSpec((1, d), lambda i, k: (0, 0)))
        args.append(gain.reshape(1, d))
    if emit_x:
        out_shape.append(jax.ShapeDtypeStruct((m, d), F32))
        out_specs.append(row_spec)
    if emit_norm:
        out_shape.append(jax.ShapeDtypeStruct((m, d), norm_dtype))
        out_specs.append(row_spec)
    outs = pl.pallas_call(
        functools.partial(_mm_res_kernel, nk=nk, emit_x=emit_x, emit_norm=emit_norm),
        grid=(m // tm, nk),
        in_specs=in_specs,
        out_specs=out_specs,
        out_shape=out_shape,
        scratch_shapes=[pltpu.VMEM((tm, d), F32)] if nk > 1 else [],
        compiler_params=_params("parallel", "arbitrary"),
        name="matmul_res",
    )(*args)
    outs = list(outs)
    x_new = outs.pop(0) if emit_x else None
    hn = outs.pop(0) if emit_norm else None
    return x_new, hn


def _gate_kernel(u_ref, v_ref, g_ref, w_ref, b_ref, p_ref, *z_refs, L, sub, nchunk):
    v = v_ref[...]
    ms = jnp.mean(v * v, axis=-1, keepdims=True)
    z = v * lax.rsqrt(ms + RMS_EPS) * g_ref[...]
    if z_refs:
        z_refs[0][...] = z
    zb = z.astype(BF16)
    gdim = v.shape[1] // SGU_GROUPS
    row = lax.broadcasted_iota(jnp.int32, (L, L), 0)
    col = lax.broadcasted_iota(jnp.int32, (L, L), 1)
    mask = jnp.logical_and(row // sub == col // sub, col % sub <= row % sub)
    for g in range(SGU_GROUPS):
        wg = jnp.where(mask, w_ref[g], 0.0).astype(BF16)
        cs = slice(g * gdim, (g + 1) * gdim)
        for c in range(nchunk):
            rs = slice(c * L, (c + 1) * L)
            s = jnp.dot(wg, zb[rs, cs], preferred_element_type=F32) + b_ref[:, cs]
            p_ref[rs, cs] = (u_ref[rs, cs] * s).astype(BF16)


def spatial_gate(uv, g_v, w_tile, b_full, *, sub, emit_z):
    m, e2 = uv.shape
    e = e2 // 2
    L = w_tile.shape[1]
    rows = min(m, 512)
    assert m % rows == 0 and rows % L == 0
    nchunk = rows // L
    out_shape = [jax.ShapeDtypeStruct((m, e), BF16)]
    out_specs = [pl.BlockSpec((rows, e), lambda i: (i, 0))]
    if emit_z:
        out_shape.append(jax.ShapeDtypeStruct((m, e), F32))
        out_specs.append(pl.BlockSpec((rows, e), lambda i: (i, 0)))
    outs = pl.pallas_call(
        functools.partial(_gate_kernel, L=L, sub=sub, nchunk=nchunk),
        grid=(m // rows,),
        in_specs=[pl.BlockSpec((rows, e), lambda i: (i, 0)),
                  pl.BlockSpec((rows, e), lambda i: (i, 1)),
                  pl.BlockSpec((1, e), lambda i: (0, 0)),
                  pl.BlockSpec(w_tile.shape, lambda i: (0, 0, 0)),
                  pl.BlockSpec((L, e), lambda i: (0, 0))],
        out_specs=out_specs,
        out_shape=out_shape,
        compiler_params=_params("parallel"),
        name="spatial_gate",
    )(uv, uv, g_v.reshape(1, e), w_tile, b_full)
    return (outs[0], outs[1]) if emit_z else (outs[0], None)


def _sb_core(z, mask, c, tri, key_axis):
    lr = -(jnp.maximum(z, 0.0) + jnp.log(1.0 + jnp.exp(-jnp.abs(z))))
    if mask is not None:
        lr = jnp.where(mask, lr, 0.0)
    hi = lr.astype(BF16)
    hilo = jnp.concatenate([hi, (lr - hi.astype(F32)).astype(BF16)], axis=key_axis)
    if key_axis == 1:
        incl = jnp.dot(hilo, tri, preferred_element_type=F32)
        total = incl[:, 0:1]
    else:
        incl = jnp.dot(tri, hilo, preferred_element_type=F32)
        total = incl[0:1, :]
    a = jnp.exp(z + incl + c)
    if mask is not None:
        a = jnp.where(mask, a, 0.0)
    return a, c + total


def _suffix_ones(tk, key_axis):
    shape = (2 * tk, tk) if key_axis == 1 else (tk, 2 * tk)
    j = lax.broadcasted_iota(jnp.int32, shape, 1 - key_axis) % tk
    s = lax.broadcasted_iota(jnp.int32, shape, key_axis)
    return (j >= s).astype(BF16)


_NT = (((1,), (1,)), ((), ()))


def _attn_prompt_kernel(bias_ref, q_ref, k_ref, v_ref, o_ref, kb_ref, vb_ref, uu_ref, *, tq, tk):
    h = pl.program_id(1)
    qs = pl.program_id(2)
    nsub = tq // tk

    @pl.when(qs == 0)
    def _():
        kb_ref[...] = k_ref[...].astype(BF16)
        vb_ref[...] = v_ref[...].astype(BF16)
        uu_ref[...] = _suffix_ones(tk, 1)

    q = q_ref[...]
    bias = bias_ref[h]
    scale = HEAD_DIM ** -0.5
    row = lax.broadcasted_iota(jnp.int32, (tq, tk), 0)
    col = lax.broadcasted_iota(jnp.int32, (tq, tk), 1)

    def block(kj, diag, c, acc):
        start = pl.multiple_of(kj * tk, tk)
        kblk = kb_ref[pl.ds(start, tk), :]
        vblk = vb_ref[pl.ds(start, tk), :]
        z = lax.dot_general(q, kblk, _NT, preferred_element_type=F32) * scale + bias
        mask = None if diag is None else (col + diag * tk < row)
        a, c = _sb_core(z, mask, c, uu_ref[...], 1)
        acc = acc + jnp.dot(a.astype(BF16), vblk, preferred_element_type=F32)
        return c, acc

    def group(first, diag, c, acc):
        for d in reversed(range(nsub)):
            c, acc = block(first + d, d if diag else None, c, acc)
        return c, acc

    c, acc = group(qs * nsub, True, jnp.zeros((tq, 1), F32), jnp.zeros((tq, HEAD_DIM), F32))
    c, acc = lax.fori_loop(0, qs, lambda it, ca: group((qs - 1 - it) * nsub, False, *ca), (c, acc))
    o_ref[...] = acc.astype(o_ref.dtype)


def attention_prompt(q, k, v, bias, *, tq=512, tk=256):
    b, t, _ = q.shape
    tq = min(tq, t)
    tk = min(tk, tq)
    assert t % tq == 0 and tq % tk == 0
    return pl.pallas_call(
        functools.partial(_attn_prompt_kernel, tq=tq, tk=tk),
        grid=(b, N_HEADS, t // tq),
        in_specs=[pl.BlockSpec(memory_space=pltpu.SMEM),
                  pl.BlockSpec((None, tq, HEAD_DIM), lambda bi, h, qi: (bi, qi, h)),
                  pl.BlockSpec((None, t, HEAD_DIM), lambda bi, h, qi: (bi, 0, h)),
                  pl.BlockSpec((None, t, HEAD_DIM), lambda bi, h, qi: (bi, 0, h))],
        out_specs=pl.BlockSpec((None, tq, HEAD_DIM), lambda bi, h, qi: (bi, qi, h)),
        out_shape=jax.ShapeDtypeStruct(q.shape, BF16),
        scratch_shapes=[pltpu.VMEM((t, HEAD_DIM), BF16),
                        pltpu.VMEM((t, HEAD_DIM), BF16),
                        pltpu.VMEM((2 * tk, tk), BF16)],
        compiler_params=_params("parallel", "parallel", "arbitrary"),
        name="attention_prompt",
    )(bias, q, k, v)


def _attn_sample_kernel(pt_ref, qt_ref, kn_ref, vn_ref, *rest, tq, page, ppstep):
    del pt_ref
    kc_refs = rest[:ppstep]
    vc_refs = rest[ppstep:2 * ppstep]
    bias_ref, o_ref, qbd_ref, acc_ref, c_ref, ll_ref, kb_ref = rest[2 * ppstep:]
    j = pl.program_id(1)
    cols = N_HEADS * tq
    d = N_HEADS * HEAD_DIM
    scale = HEAD_DIM ** -0.5

    @pl.when(j == 0)
    def _():
        rh = lax.broadcasted_iota(jnp.int32, (d, cols), 0) // HEAD_DIM
        ch = lax.broadcasted_iota(jnp.int32, (d, cols), 1) // tq
        qbd_ref[...] = jnp.where(rh == ch, qt_ref[...], 0.0).astype(BF16)
        acc_ref[...] = jnp.zeros_like(acc_ref)
        c_ref[...] = jnp.zeros_like(c_ref)
        ll_ref[...] = _suffix_ones(page, 0)

    def attend(kb, v_head, n, masked):
        z_all = jnp.dot(kb, qbd_ref[...], preferred_element_type=F32)
        mask = None
        if masked:
            s_idx = lax.broadcasted_iota(jnp.int32, (page, cols), 0)
            t_idx = lax.broadcasted_iota(jnp.int32, (page, cols), 1) % tq
            mask = s_idx < t_idx
        c = c_ref[...]
        p_parts = []
        for i in range(n):
            z = z_all[i * page:(i + 1) * page] * scale + bias_ref[...]
            a, c = _sb_core(z, mask, c, ll_ref[...], 0)
            p_parts.append(a.T)
        c_ref[...] = c
        p = p_parts[0] if n == 1 else jnp.concatenate(p_parts, axis=1)
        for h in range(N_HEADS):
            rs = slice(h * tq, (h + 1) * tq)
            acc_ref[rs, :] += jnp.dot(p[rs].astype(BF16), v_head(h), preferred_element_type=F32)

    @pl.when(j == 0)
    def _():
        attend(kn_ref[...].astype(BF16),
               lambda h: vn_ref[:, h * HEAD_DIM:(h + 1) * HEAD_DIM].astype(BF16), 1, True)

    def head_rows(ref, h):
        return ref[pl.ds(h, page, stride=N_HEADS), :].astype(BF16)

    @pl.when(j > 0)
    def _():
        for i in range(ppstep):
            for h in range(N_HEADS):
                kb_ref[i * page:(i + 1) * page, h * HEAD_DIM:(h + 1) * HEAD_DIM] = head_rows(kc_refs[i], h)

        def v_head(h):
            parts = [head_rows(vc_refs[i], h) for i in range(ppstep)]
            return parts[0] if ppstep == 1 else jnp.concatenate(parts, axis=0)

        attend(kb_ref[...], v_head, ppstep, False)

    @pl.when(j == pl.num_programs(1) - 1)
    def _():
        for h in range(N_HEADS):
            o_ref[:, h * HEAD_DIM:(h + 1) * HEAD_DIM] = acc_ref[h * tq:(h + 1) * tq, :]


def attention_sample(q, k_new, v_new, cache_k, cache_v, page_table, bias, *, ppstep=4):
    db, tq, d = q.shape
    page = cache_k.shape[1] // N_HEADS
    n_pages = page_table.shape[1]
    ppstep = min(ppstep, n_pages)
    assert n_pages % ppstep == 0
    cols = N_HEADS * tq
    pad = ((0, 0), (0, page - tq), (0, 0))
    kn = jnp.pad(k_new, pad)
    vn = jnp.pad(v_new, pad)
    qt = jnp.tile(jnp.swapaxes(q, 1, 2), (1, 1, N_HEADS))
    bias_row = jnp.repeat(bias, tq).reshape(1, cols)

    def seq_block(bi, j, pt):
        return (bi, 0, 0)

    def page_block(i):
        def index(bi, j, pt):
            return (pt[bi, n_pages - 1 - ((jnp.maximum(j, 1) - 1) * ppstep + i)], 0, 0)
        return pl.BlockSpec((None, page * N_HEADS, HEAD_DIM), index)

    grid_spec = pltpu.PrefetchScalarGridSpec(
        num_scalar_prefetch=1,
        grid=(db, n_pages // ppstep + 1),
        in_specs=([pl.BlockSpec((None, d, cols), seq_block),
                   pl.BlockSpec((None, page, d), seq_block),
                   pl.BlockSpec((None, page, d), seq_block)]
                  + [page_block(i) for i in range(ppstep)] * 2
                  + [pl.BlockSpec((1, cols), lambda bi, j, pt: (0, 0))]),
        out_specs=pl.BlockSpec((None, tq, d), seq_block),
        scratch_shapes=[pltpu.VMEM((d, cols), BF16),
                        pltpu.VMEM((cols, HEAD_DIM), F32),
                        pltpu.VMEM((1, cols), F32),
                        pltpu.VMEM((page, 2 * page), BF16),
                        pltpu.VMEM((ppstep * page, d), BF16)],
    )
    return pl.pallas_call(
        functools.partial(_attn_sample_kernel, tq=tq, page=page, ppstep=ppstep),
        grid_spec=grid_spec,
        out_shape=jax.ShapeDtypeStruct((db, tq, d), F32),
        compiler_params=_params("parallel", "arbitrary"),
        name="attention_sample",
    )(page_table, qt, kn, vn, *([cache_k] * ppstep), *([cache_v] * ppstep), bias_row)


def _trunk(x3, past, wts):
    b, t, d = x3.shape
    m = b * t
    x = x3.reshape(m, d)
    n_a = wts["a_w_in"].shape[0]
    depth = wts["w_ffn_up"].shape[0]

    sub = min(t, CHUNK)
    L = sub if sub == CHUNK else min(m, 64)
    reps = L // sub

    def ffn(x, h, layer):
        hh = matmul_ws(h, wts["w_ffn_up"], layer, "relu2")
        if layer + 1 < depth:
            return matmul_res(hh, wts["w_ffn_down"], layer, x, gain=wts["norm_mix"][layer + 1])
        return matmul_res(hh, wts["w_ffn_down"], layer, x, gain=wts["norm_final"],
                          norm_dtype=F32, emit_x=False)

    z_states = []
    h = rmsnorm(x, wts["norm_mix"][0], BF16)
    for layer in range(n_a):
        uv = matmul_ws(h, wts["a_w_in"], layer, "f32")
        w_tile = jnp.tile(wts["a_w_spatial"][layer][:, :sub, :sub], (1, reps, reps))
        b_rows = jnp.tile(wts["a_b_spatial"][layer][:, :sub].T, (reps, 1))
        b_full = jnp.repeat(b_rows, d // SGU_GROUPS, axis=1)
        p, z = spatial_gate(uv, wts["a_norm_v"][layer], w_tile, b_full,
                            sub=sub, emit_z=past is not None)
        z_states.append(z)
        x, h = matmul_res(p, wts["a_w_out"], layer, x, gain=wts["norm_ffn"][layer])
        x, h = ffn(x, h, layer)

    hd = N_HEADS * HEAD_DIM
    h_kv = rmsnorm(x, wts["kv_norm"], BF16)
    k_new = matmul_ws(h_kv, wts["w_kv"][None], 0, "f32", col0=0, n=hd)
    v_new = matmul_ws(h_kv, wts["w_kv"][None], 0, "f32", col0=hd, n=hd)

    for layer in range(n_a, depth):
        j = layer - n_a
        if past is None:
            q = matmul_ws(h, wts["b_w_q"], j, "bf16")
            o = attention_prompt(q.reshape(b, t, hd), k_new.reshape(b, t, hd),
                                 v_new.reshape(b, t, hd), wts["b_logit_bias"][j])
            o = o.reshape(m, hd)
        else:
            cache_k, cache_v, page_table = past
            q = matmul_ws(h, wts["b_w_q"], j, "f32")
            o = attention_sample(q.reshape(b, t, hd), k_new.reshape(b, t, hd),
                                 v_new.reshape(b, t, hd), cache_k, cache_v, page_table,
                                 wts["b_logit_bias"][j])
            o = o.reshape(m, hd).astype(BF16)
        x, h = matmul_res(o, wts["b_w_out"], j, x, gain=wts["norm_ffn"][layer])
        x, h = ffn(x, h, layer)

    y = h.reshape(b, t, d)
    k_out = k_new.reshape(b, t, N_HEADS, HEAD_DIM)
    v_out = v_new.reshape(b, t, N_HEADS, HEAD_DIM)
    return y, k_out, v_out, z_states


def kernel(x_prompt, x_sample, cache_k, cache_v, page_table, norm_mix, norm_ffn,
           w_ffn_up, w_ffn_down, a_w_in, a_norm_v, a_w_spatial, a_b_spatial, a_w_out,
           kv_norm, w_kv, b_w_q, b_logit_bias, b_w_out, norm_final):
    wts = dict(
        norm_mix=norm_mix, norm_ffn=norm_ffn,
        w_ffn_up=w_ffn_up, w_ffn_down=w_ffn_down.astype(BF16),
        a_w_in=a_w_in, a_norm_v=a_norm_v, a_w_spatial=a_w_spatial,
        a_b_spatial=a_b_spatial, a_w_out=a_w_out.astype(BF16), kv_norm=kv_norm,
        w_kv=w_kv, b_w_q=b_w_q, b_logit_bias=b_logit_bias,
        b_w_out=b_w_out.astype(BF16), norm_final=norm_final)

    y_prompt, k_prompt, v_prompt, _ = _trunk(x_prompt, None, wts)

    n_pool, page, heads, dh = cache_k.shape
    past = (cache_k.reshape(n_pool, page * heads, dh), cache_v.reshape(n_pool, page * heads, dh),
            page_table)
    y_sample, k_sample, v_sample, z_states = _trunk(x_sample, past, wts)
    db, ts, d = x_sample.shape
    sgu_v_sample = jnp.stack(z_states).reshape(len(z_states), db, ts, d)
    return (y_prompt, y_sample, k_prompt, v_prompt, k_sample, v_sample, sgu_v_sample)
```

```python
import functools

import jax
import jax.numpy as jnp
from jax import lax
from jax.experimental import pallas as pl
from jax.experimental.pallas import tpu as pltpu

N_HEADS = 16
HEAD_DIM = 128
CHUNK = 128
SGU_GROUPS = 8
RMS_EPS = 1e-6
LOG2E = 1.4426950408889634
PAGE_PITCH = 24

VMEM_LIMIT_BYTES = 56 * 1024 * 1024

F32 = jnp.float32
BF16 = jnp.bfloat16


def _params(*semantics):
    return pltpu.CompilerParams(dimension_semantics=semantics,
                                vmem_limit_bytes=VMEM_LIMIT_BYTES)


def _rmsnorm_kernel(x_ref, g_ref, o_ref):
    x = x_ref[...]
    ms = jnp.mean(x * x, axis=-1, keepdims=True)
    o_ref[...] = (x * lax.rsqrt(ms + RMS_EPS) * g_ref[...]).astype(o_ref.dtype)


def rmsnorm(x, g, out_dtype):
    m, d = x.shape
    tm = min(m, 512)
    assert m % tm == 0
    return pl.pallas_call(
        _rmsnorm_kernel,
        grid=(m // tm,),
        in_specs=[pl.BlockSpec((tm, d), lambda i: (i, 0)),
                  pl.BlockSpec((1, d), lambda i: (0, 0))],
        out_specs=pl.BlockSpec((tm, d), lambda i: (i, 0)),
        out_shape=jax.ShapeDtypeStruct((m, d), out_dtype),
        compiler_params=_params("parallel"),
        name="rmsnorm",
    )(x, g.reshape(1, d))


def _mm_ws_kernel(a_ref, w_ref, o_ref, wb_ref, *, epilogue, out_scale):
    @pl.when(pl.program_id(1) == 0)
    def _():
        wb_ref[...] = w_ref[...].astype(BF16)

    acc = jnp.dot(a_ref[...], wb_ref[...], preferred_element_type=F32)
    if epilogue == "relu2":
        h = jnp.maximum(acc, 0.0)
        acc = h * h
    if out_scale is not None:
        acc = acc * out_scale
    o_ref[...] = acc.astype(o_ref.dtype)


def matmul_ws(a, w, layer, epilogue, *, col0=0, n=None, out_scale=None):
    m, kdim = a.shape
    n = w.shape[2] if n is None else n
    tm = min(m, 1024)
    tn = min(n, 1024)
    assert m % tm == 0 and n % tn == 0 and col0 % tn == 0 and w.shape[1] == kdim
    jb = col0 // tn
    out_dtype = F32 if epilogue == "f32" else BF16
    return pl.pallas_call(
        functools.partial(_mm_ws_kernel, epilogue=epilogue, out_scale=out_scale),
        grid=(n // tn, m // tm),
        in_specs=[pl.BlockSpec((tm, kdim), lambda j, i: (i, 0)),
                  pl.BlockSpec((None, kdim, tn), lambda j, i: (layer, 0, jb + j))],
        out_specs=pl.BlockSpec((tm, tn), lambda j, i: (i, j)),
        out_shape=jax.ShapeDtypeStruct((m, n), out_dtype),
        scratch_shapes=[pltpu.VMEM((kdim, tn), BF16)],
        compiler_params=_params("parallel", "arbitrary"),
        name="matmul_ws_" + epilogue,
    )(a, w)


def _mm_res_kernel(*refs, nk, emit_x, emit_norm):
    a_ref, w_ref, r_ref = refs[:3]
    pos = 3
    g_ref = None
    if emit_norm:
        g_ref = refs[pos]
        pos += 1
    x_ref = hn_ref = None
    if emit_x:
        x_ref = refs[pos]
        pos += 1
    if emit_norm:
        hn_ref = refs[pos]
        pos += 1
    scratch = refs[pos:]

    def finish(acc):
        x = r_ref[...] + acc
        if emit_x:
            x_ref[...] = x
        if emit_norm:
            ms = jnp.mean(x * x, axis=-1, keepdims=True)
            hn_ref[...] = (x * lax.rsqrt(ms + RMS_EPS) * g_ref[...]).astype(hn_ref.dtype)

    part = jnp.dot(a_ref[...], w_ref[...], preferred_element_type=F32)
    if nk == 1:
        finish(part)
    else:
        acc_ref = scratch[0]
        k = pl.program_id(1)

        @pl.when(k == 0)
        def _():
            acc_ref[...] = part

        @pl.when(jnp.logical_and(k > 0, k < nk - 1))
        def _():
            acc_ref[...] += part

        @pl.when(k == nk - 1)
        def _():
            finish(acc_ref[...] + part)


def matmul_res(a, w, layer, res, *, gain=None, norm_dtype=BF16, emit_x=True):
    m, kdim = a.shape
    d = w.shape[2]
    tm = min(m, 512)
    tk = min(kdim, 2048)
    assert m % tm == 0 and kdim % tk == 0 and res.shape == (m, d)
    nk = kdim // tk
    emit_norm = gain is not None
    assert emit_x or emit_norm
    row_spec = pl.BlockSpec((tm, d), lambda i, k: (i, 0))
    in_specs = [pl.BlockSpec((tm, tk), lambda i, k: (i, k)),
                pl.BlockSpec((None, tk, d), lambda i, k: (layer, k, 0)),
                row_spec]
    args = [a, w, res]
    out_shape, out_specs = [], []
    if emit_norm:
        in_specs.append(pl.BlockSpec((1, d), lambda i, k: (0, 0)))
        args.append(gain.reshape(1, d))
    if emit_x:
        out_shape.append(jax.ShapeDtypeStruct((m, d), F32))
        out_specs.append(row_spec)
    if emit_norm:
        out_shape.append(jax.ShapeDtypeStruct((m, d), norm_dtype))
        out_specs.append(row_spec)
    outs = pl.pallas_call(
        functools.partial(_mm_res_kernel, nk=nk, emit_x=emit_x, emit_norm=emit_norm),
        grid=(m // tm, nk),
        in_specs=in_specs,
        out_specs=out_specs,
        out_shape=out_shape,
        scratch_shapes=[pltpu.VMEM((tm, d), F32)] if nk > 1 else [],
        compiler_params=_params("parallel", "arbitrary"),
        name="matmul_res",
    )(*args)
    outs = list(outs)
    x_new = outs.pop(0) if emit_x else None
    hn = outs.pop(0) if emit_norm else None
    return x_new, hn


def _gate_kernel(u_ref, v_ref, g_ref, w_ref, b_ref, p_ref, *z_refs, L, sub, nchunk):
    v = v_ref[...]
    ms = jnp.mean(v * v, axis=-1, keepdims=True)
    z = v * lax.rsqrt(ms + RMS_EPS) * g_ref[...]
    if z_refs:
        z_refs[0][...] = z
    zb = z.astype(BF16)
    gdim = v.shape[1] // SGU_GROUPS
    row = lax.broadcasted_iota(jnp.int32, (L, L), 0)
    col = lax.broadcasted_iota(jnp.int32, (L, L), 1)
    mask = jnp.logical_and(row // sub == col // sub, col % sub <= row % sub)
    for g in range(SGU_GROUPS):
        wg = jnp.where(mask, w_ref[g], 0.0).astype(BF16)
        cs = slice(g * gdim, (g + 1) * gdim)
        for c in range(nchunk):
            rs = slice(c * L, (c + 1) * L)
            s = jnp.dot(wg, zb[rs, cs], preferred_element_type=F32) + b_ref[:, cs]
            p_ref[rs, cs] = (u_ref[rs, cs] * s).astype(BF16)


def spatial_gate(uv, g_v, w_tile, b_full, *, sub, emit_z):
    m, e2 = uv.shape
    e = e2 // 2
    L = w_tile.shape[1]
    rows = min(m, 512)
    assert m % rows == 0 and rows % L == 0
    nchunk = rows // L
    out_shape = [jax.ShapeDtypeStruct((m, e), BF16)]
    out_specs = [pl.BlockSpec((rows, e), lambda i: (i, 0))]
    if emit_z:
        out_shape.append(jax.ShapeDtypeStruct((m, e), F32))
        out_specs.append(pl.BlockSpec((rows, e), lambda i: (i, 0)))
    outs = pl.pallas_call(
        functools.partial(_gate_kernel, L=L, sub=sub, nchunk=nchunk),
        grid=(m // rows,),
        in_specs=[pl.BlockSpec((rows, e), lambda i: (i, 0)),
                  pl.BlockSpec((rows, e), lambda i: (i, 1)),
                  pl.BlockSpec((1, e), lambda i: (0, 0)),
                  pl.BlockSpec(w_tile.shape, lambda i: (0, 0, 0)),
                  pl.BlockSpec((L, e), lambda i: (0, 0))],
        out_specs=out_specs,
        out_shape=out_shape,
        compiler_params=_params("parallel"),
        name="spatial_gate",
    )(uv, uv, g_v.reshape(1, e), w_tile, b_full)
    return (outs[0], outs[1]) if emit_z else (outs[0], None)


def _sb_core(z, mask, c, tri, key_axis):
    lr = -(jnp.maximum(z, 0.0) + jnp.log2(1.0 + jnp.exp2(-jnp.abs(z))))
    if mask is not None:
        lr = jnp.where(mask, lr, 0.0)
    hi = lr.astype(BF16)
    hilo = jnp.concatenate([hi, (lr - hi.astype(F32)).astype(BF16)], axis=key_axis)
    if key_axis == 1:
        incl = jnp.dot(hilo, tri, preferred_element_type=F32)
        total = incl[:, 0:1]
    else:
        incl = jnp.dot(tri, hilo, preferred_element_type=F32)
        total = incl[0:1, :]
    a = jnp.exp2(z + incl + c)
    if mask is not None:
        a = jnp.where(mask, a, 0.0)
    return a, c + total


def _suffix_ones(tk, key_axis):
    shape = (2 * tk, tk) if key_axis == 1 else (tk, 2 * tk)
    j = lax.broadcasted_iota(jnp.int32, shape, 1 - key_axis) % tk
    s = lax.broadcasted_iota(jnp.int32, shape, key_axis)
    return (j >= s).astype(BF16)


_NT = (((1,), (1,)), ((), ()))


def _attn_prompt_kernel(bias_ref, q_ref, k_ref, v_ref, o_ref, kb_ref, vb_ref, uu_ref, *, tq, tk, hps):
    hp = pl.program_id(1)
    qs = pl.program_id(2)
    nsub = tq // tk
    heads = [slice(s * HEAD_DIM, (s + 1) * HEAD_DIM) for s in range(hps)]

    @pl.when(qs == 0)
    def _():
        kb_ref[...] = k_ref[...].astype(BF16)
        vb_ref[...] = v_ref[...].astype(BF16)
        uu_ref[...] = _suffix_ones(tk, 1)

    q = [q_ref[:, hs] for hs in heads]
    bias = [bias_ref[hp * hps + s] * LOG2E for s in range(hps)]
    row = lax.broadcasted_iota(jnp.int32, (tq, tk), 0)
    col = lax.broadcasted_iota(jnp.int32, (tq, tk), 1)

    def block(kj, diag, carry):
        start = pl.multiple_of(kj * tk, tk)
        mask = None if diag is None else (col + diag * tk < row)
        out = []
        for s, hs in enumerate(heads):
            c, acc = carry[s]
            z = lax.dot_general(q[s], kb_ref[pl.ds(start, tk), hs], _NT,
                                preferred_element_type=F32) + bias[s]
            a, c = _sb_core(z, mask, c, uu_ref[...], 1)
            acc = acc + jnp.dot(a.astype(BF16), vb_ref[pl.ds(start, tk), hs],
                                preferred_element_type=F32)
            out.append((c, acc))
        return tuple(out)

    def group(first, diag, carry):
        for d in reversed(range(nsub)):
            carry = block(first + d, d if diag else None, carry)
        return carry

    zero = (jnp.zeros((tq, 1), F32), jnp.zeros((tq, HEAD_DIM), F32))
    carry = group(qs * nsub, True, (zero,) * hps)
    carry = lax.fori_loop(0, qs, lambda it, ca: group((qs - 1 - it) * nsub, False, ca), carry)
    for s, hs in enumerate(heads):
        o_ref[:, hs] = carry[s][1].astype(o_ref.dtype)


def attention_prompt(q, k, v, bias, *, tq=512, tk=256, hps=2):
    b, t, _ = q.shape
    tq = min(tq, t)
    tk = min(tk, tq)
    assert t % tq == 0 and tq % tk == 0 and N_HEADS % hps == 0
    w = hps * HEAD_DIM
    return pl.pallas_call(
        functools.partial(_attn_prompt_kernel, tq=tq, tk=tk, hps=hps),
        grid=(b, N_HEADS // hps, t // tq),
        in_specs=[pl.BlockSpec(memory_space=pltpu.SMEM),
                  pl.BlockSpec((None, tq, w), lambda bi, h, qi: (bi, qi, h)),
                  pl.BlockSpec((None, t, w), lambda bi, h, qi: (bi, 0, h)),
                  pl.BlockSpec((None, t, w), lambda bi, h, qi: (bi, 0, h))],
        out_specs=pl.BlockSpec((None, tq, w), lambda bi, h, qi: (bi, qi, h)),
        out_shape=jax.ShapeDtypeStruct(q.shape, BF16),
        scratch_shapes=[pltpu.VMEM((t, w), BF16),
                        pltpu.VMEM((t, w), BF16),
                        pltpu.VMEM((2 * tk, tk), BF16)],
        compiler_params=_params("parallel", "parallel", "arbitrary"),
        name="attention_prompt",
    )(bias, q, k, v)


def _attn_sample_kernel(pt_ref, qt_ref, kn_ref, vn_ref, bias_ref, ck_hbm, cv_hbm, o_ref,
                        qbd_ref, acc_ref, c_ref, ll_ref, kb_ref, kbuf, vbuf, sems,
                        *, tq, page, ppstep, n_pages):
    bi = pl.program_id(0)
    j = pl.program_id(1)
    nsteps = pl.num_programs(1)
    cols = N_HEADS * tq
    d = N_HEADS * HEAD_DIM

    def page_copies(step, slot):
        copies = []
        for i in range(ppstep):
            pidx = pt_ref[bi, n_pages - 1 - ((step - 1) * ppstep + i)]
            for t, (hbm, buf) in enumerate(((ck_hbm, kbuf), (cv_hbm, vbuf))):
                copies.append(pltpu.make_async_copy(
                    hbm.at[pidx], buf.at[slot, i, :, pl.ds(0, N_HEADS), :], sems.at[slot, t, i]))
        return copies

    @pl.when(j == 0)
    def _():
        for cp in page_copies(1, 1):
            cp.start()
        rh = lax.broadcasted_iota(jnp.int32, (d, cols), 0) // HEAD_DIM
        ch = lax.broadcasted_iota(jnp.int32, (d, cols), 1) // tq
        qbd_ref[...] = jnp.where(rh == ch, qt_ref[...], 0.0).astype(BF16)
        acc_ref[...] = jnp.zeros_like(acc_ref)
        c_ref[...] = jnp.zeros_like(c_ref)
        ll_ref[...] = _suffix_ones(page, 0)

    def attend(kb, v_head, n, masked):
        z_all = jnp.dot(kb, qbd_ref[...], preferred_element_type=F32)
        mask = None
        if masked:
            s_idx = lax.broadcasted_iota(jnp.int32, (page, cols), 0)
            t_idx = lax.broadcasted_iota(jnp.int32, (page, cols), 1) % tq
            mask = s_idx < t_idx
        c = c_ref[...]
        p_parts = []
        for i in range(n):
            z = z_all[i * page:(i + 1) * page] + bias_ref[...] * LOG2E
            a, c = _sb_core(z, mask, c, ll_ref[...], 0)
            p_parts.append(a.T)
        c_ref[...] = c
        p = p_parts[0] if n == 1 else jnp.concatenate(p_parts, axis=1)
        for h in range(N_HEADS):
            rs = slice(h * tq, (h + 1) * tq)
            acc_ref[rs, :] += jnp.dot(p[rs].astype(BF16), v_head(h), preferred_element_type=F32)

    @pl.when(j == 0)
    def _():
        attend(kn_ref[...].astype(BF16),
               lambda h: vn_ref[:, h * HEAD_DIM:(h + 1) * HEAD_DIM].astype(BF16), 1, True)

    @pl.when(j > 0)
    def _():
        slot = j % 2

        @pl.when(j + 1 < nsteps)
        def _():
            for cp in page_copies(j + 1, 1 - slot):
                cp.start()

        for cp in page_copies(j, slot):
            cp.wait()

        rows_per_page = page * PAGE_PITCH
        kflat = kbuf.reshape(2 * ppstep * rows_per_page, HEAD_DIM)
        vflat = vbuf.reshape(2 * ppstep * rows_per_page, HEAD_DIM)

        def head_rows(flat, i, h):
            start = (slot * ppstep + i) * rows_per_page + h
            return flat[pl.ds(start, page, stride=PAGE_PITCH), :].astype(BF16)

        for i in range(ppstep):
            for h in range(N_HEADS):
                kb_ref[i * page:(i + 1) * page, h * HEAD_DIM:(h + 1) * HEAD_DIM] = head_rows(kflat, i, h)

        def v_head(h):
            parts = [head_rows(vflat, i, h) for i in range(ppstep)]
            return parts[0] if ppstep == 1 else jnp.concatenate(parts, axis=0)

        attend(kb_ref[...], v_head, ppstep, False)

    @pl.when(j == pl.num_programs(1) - 1)
    def _():
        for h in range(N_HEADS):
            o_ref[:, h * HEAD_DIM:(h + 1) * HEAD_DIM] = acc_ref[h * tq:(h + 1) * tq, :]


def attention_sample(q, k_new, v_new, cache_k, cache_v, page_table, bias, *, ppstep=4):
    db, tq, d = q.shape
    page = cache_k.shape[1]
    n_pages = page_table.shape[1]
    ppstep = min(ppstep, n_pages)
    assert n_pages % ppstep == 0
    cols = N_HEADS * tq
    pad = ((0, 0), (0, page - tq), (0, 0))
    kn = jnp.pad(k_new, pad)
    vn = jnp.pad(v_new, pad)
    qt = jnp.tile(jnp.swapaxes(q, 1, 2), (1, 1, N_HEADS))
    bias_row = jnp.repeat(bias, tq).reshape(1, cols)

    def seq_block(bi, j, pt):
        return (bi, 0, 0)

    page_buf = pltpu.VMEM((2, ppstep, page, PAGE_PITCH, HEAD_DIM), F32)
    grid_spec = pltpu.PrefetchScalarGridSpec(
        num_scalar_prefetch=1,
        grid=(db, n_pages // ppstep + 1),
        in_specs=[pl.BlockSpec((None, d, cols), seq_block),
                  pl.BlockSpec((None, page, d), seq_block),
                  pl.BlockSpec((None, page, d), seq_block),
                  pl.BlockSpec((1, cols), lambda bi, j, pt: (0, 0)),
                  pl.BlockSpec(memory_space=pl.ANY),
                  pl.BlockSpec(memory_space=pl.ANY)],
        out_specs=pl.BlockSpec((None, tq, d), seq_block),
        scratch_shapes=[pltpu.VMEM((d, cols), BF16),
                        pltpu.VMEM((cols, HEAD_DIM), F32),
                        pltpu.VMEM((1, cols), F32),
                        pltpu.VMEM((page, 2 * page), BF16),
                        pltpu.VMEM((ppstep * page, d), BF16),
                        page_buf, page_buf,
                        pltpu.SemaphoreType.DMA((2, 2, ppstep))],
    )
    return pl.pallas_call(
        functools.partial(_attn_sample_kernel, tq=tq, page=page, ppstep=ppstep, n_pages=n_pages),
        grid_spec=grid_spec,
        out_shape=jax.ShapeDtypeStruct((db, tq, d), F32),
        compiler_params=_params("arbitrary", "arbitrary"),
        name="attention_sample",
    )(page_table, qt, kn, vn, bias_row, cache_k, cache_v)


def _trunk(x3, past, wts):
    b, t, d = x3.shape
    m = b * t
    x = x3.reshape(m, d)
    n_a = wts["a_w_in"].shape[0]
    depth = wts["w_ffn_up"].shape[0]

    sub = min(t, CHUNK)
    L = sub if sub == CHUNK else min(m, 64)
    reps = L // sub

    def ffn(x, h, layer):
        hh = matmul_ws(h, wts["w_ffn_up"], layer, "relu2")
        if layer + 1 < depth:
            return matmul_res(hh, wts["w_ffn_down"], layer, x, gain=wts["norm_mix"][layer + 1])
        return matmul_res(hh, wts["w_ffn_down"], layer, x, gain=wts["norm_final"],
                          norm_dtype=F32, emit_x=False)

    z_states = []
    h = rmsnorm(x, wts["norm_mix"][0], BF16)
    for layer in range(n_a):
        uv = matmul_ws(h, wts["a_w_in"], layer, "f32")
        w_tile = jnp.tile(wts["a_w_spatial"][layer][:, :sub, :sub], (1, reps, reps))
        b_rows = jnp.tile(wts["a_b_spatial"][layer][:, :sub].T, (reps, 1))
        b_full = jnp.repeat(b_rows, d // SGU_GROUPS, axis=1)
        p, z = spatial_gate(uv, wts["a_norm_v"][layer], w_tile, b_full,
                            sub=sub, emit_z=past is not None)
        z_states.append(z)
        x, h = matmul_res(p, wts["a_w_out"], layer, x, gain=wts["norm_ffn"][layer])
        x, h = ffn(x, h, layer)

    hd = N_HEADS * HEAD_DIM
    h_kv = rmsnorm(x, wts["kv_norm"], BF16)
    k_new = matmul_ws(h_kv, wts["w_kv"][None], 0, "f32", col0=0, n=hd)
    v_new = matmul_ws(h_kv, wts["w_kv"][None], 0, "f32", col0=hd, n=hd)

    q_scale = LOG2E * HEAD_DIM ** -0.5
    for layer in range(n_a, depth):
        j = layer - n_a
        if past is None:
            q = matmul_ws(h, wts["b_w_q"], j, "bf16", out_scale=q_scale)
            o = attention_prompt(q.reshape(b, t, hd), k_new.reshape(b, t, hd),
                                 v_new.reshape(b, t, hd), wts["b_logit_bias"][j])
            o = o.reshape(m, hd)
        else:
            cache_k, cache_v, page_table = past
            q = matmul_ws(h, wts["b_w_q"], j, "f32", out_scale=q_scale)
            o = attention_sample(q.reshape(b, t, hd), k_new.reshape(b, t, hd),
                                 v_new.reshape(b, t, hd), cache_k, cache_v, page_table,
                                 wts["b_logit_bias"][j])
            o = o.reshape(m, hd).astype(BF16)
        x, h = matmul_res(o, wts["b_w_out"], j, x, gain=wts["norm_ffn"][layer])
        x, h = ffn(x, h, layer)

    y = h.reshape(b, t, d)
    k_out = k_new.reshape(b, t, N_HEADS, HEAD_DIM)
    v_out = v_new.reshape(b, t, N_HEADS, HEAD_DIM)
    return y, k_out, v_out, z_states


def kernel(x_prompt, x_sample, cache_k, cache_v, page_table, norm_mix, norm_ffn,
           w_ffn_up, w_ffn_down, a_w_in, a_norm_v, a_w_spatial, a_b_spatial, a_w_out,
           kv_norm, w_kv, b_w_q, b_logit_bias, b_w_out, norm_final):
    wts = dict(
        norm_mix=norm_mix, norm_ffn=norm_ffn,
        w_ffn_up=w_ffn_up, w_ffn_down=w_ffn_down.astype(BF16),
        a_w_in=a_w_in, a_norm_v=a_norm_v, a_w_spatial=a_w_spatial,
        a_b_spatial=a_b_spatial, a_w_out=a_w_out.astype(BF16), kv_norm=kv_norm,
        w_kv=w_kv, b_w_q=b_w_q, b_logit_bias=b_logit_bias,
        b_w_out=b_w_out.astype(BF16), norm_final=norm_final)

    y_prompt, k_prompt, v_prompt, _ = _trunk(x_prompt, None, wts)

    y_sample, k_sample, v_sample, z_states = _trunk(x_sample, (cache_k, cache_v, page_table), wts)
    db, ts, d = x_sample.shape
    sgu_v_sample = jnp.stack(z_states).reshape(len(z_states), db, ts, d)
    return (y_prompt, y_sample, k_prompt, v_prompt, k_sample, v_sample, sgu_v_sample)
```

```python
import functools

import jax
import jax.numpy as jnp
from jax import lax
from jax.experimental import pallas as pl
from jax.experimental.pallas import tpu as pltpu

N_HEADS = 16
HEAD_DIM = 128
CHUNK = 128
SGU_GROUPS = 8
RMS_EPS = 1e-6
LOG2E = 1.4426950408889634
PAGE_PITCH = 24
PAGE_SLOTS = 3

VMEM_LIMIT_BYTES = 56 * 1024 * 1024

F32 = jnp.float32
BF16 = jnp.bfloat16


def _params(*semantics):
    return pltpu.CompilerParams(dimension_semantics=semantics,
                                vmem_limit_bytes=VMEM_LIMIT_BYTES)


def _rmsnorm_kernel(x_ref, g_ref, o_ref):
    x = x_ref[...]
    ms = jnp.mean(x * x, axis=-1, keepdims=True)
    o_ref[...] = (x * lax.rsqrt(ms + RMS_EPS) * g_ref[...]).astype(o_ref.dtype)


def rmsnorm(x, g, out_dtype):
    m, d = x.shape
    tm = min(m, 512)
    assert m % tm == 0
    return pl.pallas_call(
        _rmsnorm_kernel,
        grid=(m // tm,),
        in_specs=[pl.BlockSpec((tm, d), lambda i: (i, 0)),
                  pl.BlockSpec((1, d), lambda i: (0, 0))],
        out_specs=pl.BlockSpec((tm, d), lambda i: (i, 0)),
        out_shape=jax.ShapeDtypeStruct((m, d), out_dtype),
        compiler_params=_params("parallel"),
        name="rmsnorm",
    )(x, g.reshape(1, d))


def _mm_ws_kernel(*refs, epilogue, out_scale, nm, joint):
    if joint:
        a_ref, a2_ref, w_ref, o_ref, o2_ref, wb_ref = refs
    else:
        a_ref, w_ref, o_ref, wb_ref = refs
    i = pl.program_id(1)

    @pl.when(i == 0)
    def _():
        wb_ref[...] = w_ref[...].astype(BF16)

    def compute(a):
        acc = jnp.dot(a, wb_ref[...], preferred_element_type=F32)
        if epilogue == "relu2":
            h = jnp.maximum(acc, 0.0)
            acc = h * h
        if out_scale is not None:
            acc = acc * out_scale
        return acc

    if not joint:
        o_ref[...] = compute(a_ref[...]).astype(o_ref.dtype)
    else:
        @pl.when(i < nm)
        def _():
            o_ref[...] = compute(a_ref[...]).astype(o_ref.dtype)

        @pl.when(i == nm)
        def _():
            o2_ref[...] = compute(a2_ref[...]).astype(o2_ref.dtype)


def matmul_ws(a, w, layer, epilogue, *, a2=None, col0=0, n=None, out_scale=None):
    m, kdim = a.shape
    n = w.shape[2] if n is None else n
    tm = min(m, 1024)
    tn = min(n, 1024)
    assert m % tm == 0 and n % tn == 0 and col0 % tn == 0 and w.shape[1] == kdim
    jb = col0 // tn
    nm = m // tm
    joint = a2 is not None
    out_dtype = F32 if epilogue == "f32" else BF16
    last = nm - 1
    in_specs = [pl.BlockSpec((tm, kdim), lambda j, i: (jnp.minimum(i, last), 0))]
    out_specs = [pl.BlockSpec((tm, tn), lambda j, i: (jnp.minimum(i, last), j))]
    out_shape = [jax.ShapeDtypeStruct((m, n), out_dtype)]
    args = [a]
    if joint:
        m2 = a2.shape[0]
        in_specs.append(pl.BlockSpec((m2, kdim), lambda j, i: (0, 0)))
        out_specs.append(pl.BlockSpec((m2, tn), lambda j, i: (0, j)))
        out_shape.append(jax.ShapeDtypeStruct((m2, n), out_dtype))
        args.append(a2)
    in_specs.append(pl.BlockSpec((None, kdim, tn), lambda j, i: (layer, 0, jb + j)))
    args.append(w)
    outs = pl.pallas_call(
        functools.partial(_mm_ws_kernel, epilogue=epilogue, out_scale=out_scale, nm=nm, joint=joint),
        grid=(n // tn, nm + int(joint)),
        in_specs=in_specs,
        out_specs=out_specs,
        out_shape=out_shape,
        scratch_shapes=[pltpu.VMEM((kdim, tn), BF16)],
        compiler_params=_params("parallel", "arbitrary"),
        name="matmul_ws_" + epilogue,
    )(*args)
    return tuple(outs) if joint else outs[0]


def _mm_res_kernel(*refs, nk, emit_x, emit_norm):
    a_ref, w_ref, r_ref = refs[:3]
    pos = 3
    g_ref = None
    if emit_norm:
        g_ref = refs[pos]
        pos += 1
    x_ref = hn_ref = None
    if emit_x:
        x_ref = refs[pos]
        pos += 1
    if emit_norm:
        hn_ref = refs[pos]
        pos += 1
    scratch = refs[pos:]

    def finish(acc):
        x = r_ref[...] + acc
        if emit_x:
            x_ref[...] = x
        if emit_norm:
            ms = jnp.mean(x * x, axis=-1, keepdims=True)
            hn_ref[...] = (x * lax.rsqrt(ms + RMS_EPS) * g_ref[...]).astype(hn_ref.dtype)

    part = jnp.dot(a_ref[...], w_ref[...], preferred_element_type=F32)
    if nk == 1:
        finish(part)
    else:
        acc_ref = scratch[0]
        k = pl.program_id(1)

        @pl.when(k == 0)
        def _():
            acc_ref[...] = part

        @pl.when(jnp.logical_and(k > 0, k < nk - 1))
        def _():
            acc_ref[...] += part

        @pl.when(k == nk - 1)
        def _():
            finish(acc_ref[...] + part)


def matmul_res(a, w, layer, res, *, gain=None, norm_dtype=BF16, emit_x=True):
    m, kdim = a.shape
    d = w.shape[2]
    tm = min(m, 512)
    tk = min(kdim, 2048)
    assert m % tm == 0 and kdim % tk == 0 and res.shape == (m, d)
    nk = kdim // tk
    emit_norm = gain is not None
    assert emit_x or emit_norm
    row_spec = pl.BlockSpec((tm, d), lambda i, k: (i, 0))
    in_specs = [pl.BlockSpec((tm, tk), lambda i, k: (i, k)),
                pl.BlockSpec((None, tk, d), lambda i, k: (layer, k, 0)),
                row_spec]
    args = [a, w, res]
    out_shape, out_specs = [], []
    if emit_norm:
        in_specs.append(pl.BlockSpec((1, d), lambda i, k: (0, 0)))
        args.append(gain.reshape(1, d))
    if emit_x:
        out_shape.append(jax.ShapeDtypeStruct((m, d), F32))
        out_specs.append(row_spec)
    if emit_norm:
        out_shape.append(jax.ShapeDtypeStruct((m, d), norm_dtype))
        out_specs.append(row_spec)
    outs = pl.pallas_call(
        functools.partial(_mm_res_kernel, nk=nk, emit_x=emit_x, emit_norm=emit_norm),
        grid=(m // tm, nk),
        in_specs=in_specs,
        out_specs=out_specs,
        out_shape=out_shape,
        scratch_shapes=[pltpu.VMEM((tm, d), F32)] if nk > 1 else [],
        compiler_params=_params("parallel", "arbitrary"),
        name="matmul_res",
    )(*args)
    outs = list(outs)
    x_new = outs.pop(0) if emit_x else None
    hn = outs.pop(0) if emit_norm else None
    return x_new, hn


def _gate_kernel(u_ref, v_ref, g_ref, w_ref, b_ref, p_ref, *z_refs, L, sub, nchunk):
    v = v_ref[...]
    ms = jnp.mean(v * v, axis=-1, keepdims=True)
    z = v * lax.rsqrt(ms + RMS_EPS) * g_ref[...]
    if z_refs:
        z_refs[0][...] = z
    zb = z.astype(BF16)
    gdim = v.shape[1] // SGU_GROUPS
    row = lax.broadcasted_iota(jnp.int32, (L, L), 0)
    col = lax.broadcasted_iota(jnp.int32, (L, L), 1)
    mask = jnp.logical_and(row // sub == col // sub, col % sub <= row % sub)
    for g in range(SGU_GROUPS):
        wg = jnp.where(mask, w_ref[g], 0.0).astype(BF16)
        cs = slice(g * gdim, (g + 1) * gdim)
        for c in range(nchunk):
            rs = slice(c * L, (c + 1) * L)
            s = jnp.dot(wg, zb[rs, cs], preferred_element_type=F32) + b_ref[:, cs]
            p_ref[rs, cs] = (u_ref[rs, cs] * s).astype(BF16)


def spatial_gate(uv, g_v, w_tile, b_full, *, sub, emit_z):
    m, e2 = uv.shape
    e = e2 // 2
    L = w_tile.shape[1]
    rows = min(m, 512)
    assert m % rows == 0 and rows % L == 0
    nchunk = rows // L
    out_shape = [jax.ShapeDtypeStruct((m, e), BF16)]
    out_specs = [pl.BlockSpec((rows, e), lambda i: (i, 0))]
    if emit_z:
        out_shape.append(jax.ShapeDtypeStruct((m, e), F32))
        out_specs.append(pl.BlockSpec((rows, e), lambda i: (i, 0)))
    outs = pl.pallas_call(
        functools.partial(_gate_kernel, L=L, sub=sub, nchunk=nchunk),
        grid=(m // rows,),
        in_specs=[pl.BlockSpec((rows, e), lambda i: (i, 0)),
                  pl.BlockSpec((rows, e), lambda i: (i, 1)),
                  pl.BlockSpec((1, e), lambda i: (0, 0)),
                  pl.BlockSpec(w_tile.shape, lambda i: (0, 0, 0)),
                  pl.BlockSpec((L, e), lambda i: (0, 0))],
        out_specs=out_specs,
        out_shape=out_shape,
        compiler_params=_params("parallel"),
        name="spatial_gate",
    )(uv, uv, g_v.reshape(1, e), w_tile, b_full)
    return (outs[0], outs[1]) if emit_z else (outs[0], None)


def _sb_core(z, mask, c, tri, key_axis, split):
    lr = -(jnp.maximum(z, 0.0) + jnp.log2(1.0 + jnp.exp2(-jnp.abs(z))))
    if mask is not None:
        lr = jnp.where(mask, lr, 0.0)
    lrb = lr.astype(BF16)
    if split:
        lrb = jnp.concatenate([lrb, (lr - lrb.astype(F32)).astype(BF16)], axis=key_axis)
    if key_axis == 1:
        incl = jnp.dot(lrb, tri, preferred_element_type=F32)
        total = incl[:, 0:1]
    else:
        incl = jnp.dot(tri, lrb, preferred_element_type=F32)
        total = incl[0:1, :]
    a = jnp.exp2(z + incl + c)
    if mask is not None:
        a = jnp.where(mask, a, 0.0)
    return a, c + total


def _suffix_ones(tk, key_axis, copies):
    shape = (copies * tk, tk) if key_axis == 1 else (tk, copies * tk)
    j = lax.broadcasted_iota(jnp.int32, shape, 1 - key_axis) % tk
    s = lax.broadcasted_iota(jnp.int32, shape, key_axis)
    return (j >= s).astype(BF16)


_NT = (((1,), (1,)), ((), ()))


def _attn_prompt_kernel(bias_ref, q_ref, k_ref, v_ref, o_ref, kb_ref, vb_ref, uu_ref, *, tq, tk, hps):
    hp = pl.program_id(1)
    qs = pl.program_id(2)
    nsub = tq // tk
    heads = [slice(s * HEAD_DIM, (s + 1) * HEAD_DIM) for s in range(hps)]

    @pl.when(qs == 0)
    def _():
        kb_ref[...] = k_ref[...].astype(BF16)
        vb_ref[...] = v_ref[...].astype(BF16)
        uu_ref[...] = _suffix_ones(tk, 1, 1)

    q = [q_ref[:, hs] for hs in heads]
    bias = [bias_ref[hp * hps + s] * LOG2E for s in range(hps)]
    row = lax.broadcasted_iota(jnp.int32, (tq, tk), 0)
    col = lax.broadcasted_iota(jnp.int32, (tq, tk), 1)

    def block(kj, diag, carry):
        start = pl.multiple_of(kj * tk, tk)
        mask = None if diag is None else (col + diag * tk < row)
        out = []
        for s, hs in enumerate(heads):
            c, acc = carry[s]
            z = lax.dot_general(q[s], kb_ref[pl.ds(start, tk), hs], _NT,
                                preferred_element_type=F32) + bias[s]
            a, c = _sb_core(z, mask, c, uu_ref[...], 1, split=False)
            acc = acc + jnp.dot(a.astype(BF16), vb_ref[pl.ds(start, tk), hs],
                                preferred_element_type=F32)
            out.append((c, acc))
        return tuple(out)

    def group(first, diag, carry):
        for d in reversed(range(nsub)):
            carry = block(first + d, d if diag else None, carry)
        return carry

    zero = (jnp.zeros((tq, 1), F32), jnp.zeros((tq, HEAD_DIM), F32))
    carry = group(qs * nsub, True, (zero,) * hps)
    carry = lax.fori_loop(0, qs, lambda it, ca: group((qs - 1 - it) * nsub, False, ca), carry)
    for s, hs in enumerate(heads):
        o_ref[:, hs] = carry[s][1].astype(o_ref.dtype)


def attention_prompt(q, k, v, bias, *, tq=512, tk=256, hps=2):
    b, t, _ = q.shape
    tq = min(tq, t)
    tk = min(tk, tq)
    assert t % tq == 0 and tq % tk == 0 and N_HEADS % hps == 0
    w = hps * HEAD_DIM
    return pl.pallas_call(
        functools.partial(_attn_prompt_kernel, tq=tq, tk=tk, hps=hps),
        grid=(b, N_HEADS // hps, t // tq),
        in_specs=[pl.BlockSpec(memory_space=pltpu.SMEM),
                  pl.BlockSpec((None, tq, w), lambda bi, h, qi: (bi, qi, h)),
                  pl.BlockSpec((None, t, w), lambda bi, h, qi: (bi, 0, h)),
                  pl.BlockSpec((None, t, w), lambda bi, h, qi: (bi, 0, h))],
        out_specs=pl.BlockSpec((None, tq, w), lambda bi, h, qi: (bi, qi, h)),
        out_shape=jax.ShapeDtypeStruct(q.shape, BF16),
        scratch_shapes=[pltpu.VMEM((t, w), BF16),
                        pltpu.VMEM((t, w), BF16),
                        pltpu.VMEM((tk, tk), BF16)],
        compiler_params=_params("parallel", "parallel", "arbitrary"),
        name="attention_prompt",
    )(bias, q, k, v)


def _attn_sample_kernel(pt_ref, qt_ref, kn_ref, vn_ref, bias_ref, ck_hbm, cv_hbm, o_ref,
                        qbd_ref, acc_ref, c_ref, ll_ref, kb_ref, kbuf, vbuf, sems,
                        *, tq, page, ppstep, n_pages):
    bi = pl.program_id(0)
    j = pl.program_id(1)
    nsteps = pl.num_programs(1)
    cols = N_HEADS * tq
    d = N_HEADS * HEAD_DIM

    def page_copies(step, slot):
        copies = []
        for i in range(ppstep):
            pidx = pt_ref[bi, n_pages - 1 - ((step - 1) * ppstep + i)]
            for t, (hbm, buf) in enumerate(((ck_hbm, kbuf), (cv_hbm, vbuf))):
                copies.append(pltpu.make_async_copy(
                    hbm.at[pidx], buf.at[slot, i, :, pl.ds(0, N_HEADS), :], sems.at[slot, t, i]))
        return copies

    @pl.when(j == 0)
    def _():
        for step in range(1, PAGE_SLOTS):
            @pl.when(step < nsteps)
            def _():
                for cp in page_copies(step, step % PAGE_SLOTS):
                    cp.start()
        rh = lax.broadcasted_iota(jnp.int32, (d, cols), 0) // HEAD_DIM
        ch = lax.broadcasted_iota(jnp.int32, (d, cols), 1) // tq
        qbd_ref[...] = jnp.where(rh == ch, qt_ref[...], 0.0).astype(BF16)
        acc_ref[...] = jnp.zeros_like(acc_ref)
        c_ref[...] = jnp.zeros_like(c_ref)
        ll_ref[...] = _suffix_ones(page, 0, 2)

    def attend(kb, v_head, n, masked):
        z_all = jnp.dot(kb, qbd_ref[...], preferred_element_type=F32)
        mask = None
        if masked:
            s_idx = lax.broadcasted_iota(jnp.int32, (page, cols), 0)
            t_idx = lax.broadcasted_iota(jnp.int32, (page, cols), 1) % tq
            mask = s_idx < t_idx
        c = c_ref[...]
        p_parts = []
        for i in range(n):
            z = z_all[i * page:(i + 1) * page] + bias_ref[...] * LOG2E
            a, c = _sb_core(z, mask, c, ll_ref[...], 0, split=True)
            p_parts.append(a.T)
        c_ref[...] = c
        p = p_parts[0] if n == 1 else jnp.concatenate(p_parts, axis=1)
        for h in range(N_HEADS):
            rs = slice(h * tq, (h + 1) * tq)
            acc_ref[rs, :] += jnp.dot(p[rs].astype(BF16), v_head(h), preferred_element_type=F32)

    @pl.when(j == 0)
    def _():
        attend(kn_ref[...].astype(BF16),
               lambda h: vn_ref[:, h * HEAD_DIM:(h + 1) * HEAD_DIM].astype(BF16), 1, True)

    @pl.when(j > 0)
    def _():
        slot = j % PAGE_SLOTS
        ahead = j + PAGE_SLOTS - 1

        @pl.when(ahead < nsteps)
        def _():
            for cp in page_copies(ahead, ahead % PAGE_SLOTS):
                cp.start()

        for cp in page_copies(j, slot):
            cp.wait()

        rows_per_page = page * PAGE_PITCH
        kflat = kbuf.reshape(PAGE_SLOTS * ppstep * rows_per_page, HEAD_DIM)
        vflat = vbuf.reshape(PAGE_SLOTS * ppstep * rows_per_page, HEAD_DIM)

        def head_rows(flat, i, h):
            start = (slot * ppstep + i) * rows_per_page + h
            return flat[pl.ds(start, page, stride=PAGE_PITCH), :].astype(BF16)

        for i in range(ppstep):
            for h in range(N_HEADS):
                kb_ref[i * page:(i + 1) * page, h * HEAD_DIM:(h + 1) * HEAD_DIM] = head_rows(kflat, i, h)

        def v_head(h):
            parts = [head_rows(vflat, i, h) for i in range(ppstep)]
            return parts[0] if ppstep == 1 else jnp.concatenate(parts, axis=0)

        attend(kb_ref[...], v_head, ppstep, False)

    @pl.when(j == pl.num_programs(1) - 1)
    def _():
        for h in range(N_HEADS):
            o_ref[:, h * HEAD_DIM:(h + 1) * HEAD_DIM] = acc_ref[h * tq:(h + 1) * tq, :]


def attention_sample(q, k_new, v_new, cache_k, cache_v, page_table, bias, *, ppstep=4):
    db, tq, d = q.shape
    page = cache_k.shape[1]
    n_pages = page_table.shape[1]
    ppstep = min(ppstep, n_pages)
    assert n_pages % ppstep == 0
    cols = N_HEADS * tq
    pad = ((0, 0), (0, page - tq), (0, 0))
    kn = jnp.pad(k_new, pad)
    vn = jnp.pad(v_new, pad)
    qt = jnp.tile(jnp.swapaxes(q, 1, 2), (1, 1, N_HEADS)).astype(F32)
    bias_row = jnp.repeat(bias, tq).reshape(1, cols)

    def seq_block(bi, j, pt):
        return (bi, 0, 0)

    page_buf = pltpu.VMEM((PAGE_SLOTS, ppstep, page, PAGE_PITCH, HEAD_DIM), F32)
    grid_spec = pltpu.PrefetchScalarGridSpec(
        num_scalar_prefetch=1,
        grid=(db, n_pages // ppstep + 1),
        in_specs=[pl.BlockSpec((None, d, cols), seq_block),
                  pl.BlockSpec((None, page, d), seq_block),
                  pl.BlockSpec((None, page, d), seq_block),
                  pl.BlockSpec((1, cols), lambda bi, j, pt: (0, 0)),
                  pl.BlockSpec(memory_space=pl.ANY),
                  pl.BlockSpec(memory_space=pl.ANY)],
        out_specs=pl.BlockSpec((None, tq, d), seq_block),
        scratch_shapes=[pltpu.VMEM((d, cols), BF16),
                        pltpu.VMEM((cols, HEAD_DIM), F32),
                        pltpu.VMEM((1, cols), F32),
                        pltpu.VMEM((page, 2 * page), BF16),
                        pltpu.VMEM((ppstep * page, d), BF16),
                        page_buf, page_buf,
                        pltpu.SemaphoreType.DMA((PAGE_SLOTS, 2, ppstep))],
    )
    return pl.pallas_call(
        functools.partial(_attn_sample_kernel, tq=tq, page=page, ppstep=ppstep, n_pages=n_pages),
        grid_spec=grid_spec,
        out_shape=jax.ShapeDtypeStruct((db, tq, d), F32),
        compiler_params=_params("arbitrary", "arbitrary"),
        name="attention_sample",
    )(page_table, qt, kn, vn, bias_row, cache_k, cache_v)


def _trunk(x3s, past, wts):
    dims = [x3.shape for x3 in x3s]
    d = dims[0][2]
    ms = [bb * tt for bb, tt, _ in dims]
    x = [x3.reshape(mm, d) for x3, mm in zip(x3s, ms)]
    n_a = wts["a_w_in"].shape[0]
    depth = wts["w_ffn_up"].shape[0]
    groups = range(2)
    hd = N_HEADS * HEAD_DIM

    def unzip(pairs):
        return [p[0] for p in pairs], [p[1] for p in pairs]

    def ffn(x, h, layer):
        hh = matmul_ws(h[0], wts["w_ffn_up"], layer, "relu2", a2=h[1])
        if layer + 1 < depth:
            kw = dict(gain=wts["norm_mix"][layer + 1])
        else:
            kw = dict(gain=wts["norm_final"], norm_dtype=F32, emit_x=False)
        return unzip([matmul_res(hh[g], wts["w_ffn_down"], layer, x[g], **kw) for g in groups])

    def gate(uv, layer, g):
        t, m = dims[g][1], ms[g]
        sub = min(t, CHUNK)
        L = sub if sub == CHUNK else min(m, 64)
        reps = L // sub
        w_tile = jnp.tile(wts["a_w_spatial"][layer][:, :sub, :sub], (1, reps, reps))
        b_rows = jnp.tile(wts["a_b_spatial"][layer][:, :sub].T, (reps, 1))
        b_full = jnp.repeat(b_rows, d // SGU_GROUPS, axis=1)
        return spatial_gate(uv, wts["a_norm_v"][layer], w_tile, b_full, sub=sub, emit_z=g == 1)

    z_states = []
    h = [rmsnorm(x[g], wts["norm_mix"][0], BF16) for g in groups]
    for layer in range(n_a):
        uv = matmul_ws(h[0], wts["a_w_in"], layer, "f32", a2=h[1])
        p, z = unzip([gate(uv[g], layer, g) for g in groups])
        z_states.append(z[1])
        x, h = unzip([matmul_res(p[g], wts["a_w_out"], layer, x[g], gain=wts["norm_ffn"][layer])
                      for g in groups])
        x, h = ffn(x, h, layer)

    h_kv = [rmsnorm(x[g], wts["kv_norm"], BF16) for g in groups]
    k_new = matmul_ws(h_kv[0], wts["w_kv"][None], 0, "f32", a2=h_kv[1], col0=0, n=hd)
    v_new = matmul_ws(h_kv[0], wts["w_kv"][None], 0, "f32", a2=h_kv[1], col0=hd, n=hd)

    q_scale = LOG2E * HEAD_DIM ** -0.5
    cache_k, cache_v, page_table = past
    for layer in range(n_a, depth):
        j = layer - n_a
        q = matmul_ws(h[0], wts["b_w_q"], j, "bf16", a2=h[1], out_scale=q_scale)
        (b0, t0, _), (b1, t1, _) = dims
        o0 = attention_prompt(q[0].reshape(b0, t0, hd), k_new[0].reshape(b0, t0, hd),
                              v_new[0].reshape(b0, t0, hd), wts["b_logit_bias"][j])
        o1 = attention_sample(q[1].reshape(b1, t1, hd), k_new[1].reshape(b1, t1, hd),
                              v_new[1].reshape(b1, t1, hd), cache_k, cache_v, page_table,
                              wts["b_logit_bias"][j])
        o = [o0.reshape(ms[0], hd), o1.reshape(ms[1], hd).astype(BF16)]
        x, h = unzip([matmul_res(o[g], wts["b_w_out"], j, x[g], gain=wts["norm_ffn"][layer])
                      for g in groups])
        x, h = ffn(x, h, layer)

    y = [h[g].reshape(dims[g]) for g in groups]
    k_out = [k_new[g].reshape(dims[g][0], dims[g][1], N_HEADS, HEAD_DIM) for g in groups]
    v_out = [v_new[g].reshape(dims[g][0], dims[g][1], N_HEADS, HEAD_DIM) for g in groups]
    return y, k_out, v_out, z_states


def kernel(x_prompt, x_sample, cache_k, cache_v, page_table, norm_mix, norm_ffn,
           w_ffn_up, w_ffn_down, a_w_in, a_norm_v, a_w_spatial, a_b_spatial, a_w_out,
           kv_norm, w_kv, b_w_q, b_logit_bias, b_w_out, norm_final):
    wts = dict(
        norm_mix=norm_mix, norm_ffn=norm_ffn,
        w_ffn_up=w_ffn_up, w_ffn_down=w_ffn_down.astype(BF16),
        a_w_in=a_w_in, a_norm_v=a_norm_v, a_w_spatial=a_w_spatial,
        a_b_spatial=a_b_spatial, a_w_out=a_w_out.astype(BF16), kv_norm=kv_norm,
        w_kv=w_kv, b_w_q=b_w_q, b_logit_bias=b_logit_bias,
        b_w_out=b_w_out.astype(BF16), norm_final=norm_final)

    y, k_out, v_out, z_states = _trunk([x_prompt, x_sample], (cache_k, cache_v, page_table), wts)
    db, ts, d = x_sample.shape
    sgu_v_sample = jnp.stack(z_states).reshape(len(z_states), db, ts, d)
    return (y[0], y[1], k_out[0], v_out[0], k_out[1], v_out[1], sgu_v_sample)
```

```python
import functools

import jax
import jax.numpy as jnp
from jax import lax
from jax.experimental import pallas as pl
from jax.experimental.pallas import tpu as pltpu

N_HEADS = 16
HEAD_DIM = 128
CHUNK = 128
SGU_GROUPS = 8
RMS_EPS = 1e-6
LOG2E = 1.4426950408889634
PAGE_PITCH = 24
PAGE_SLOTS = 3

VMEM_LIMIT_BYTES = 56 * 1024 * 1024

F32 = jnp.float32
BF16 = jnp.bfloat16


def _params(*semantics):
    return pltpu.CompilerParams(dimension_semantics=semantics,
                                vmem_limit_bytes=VMEM_LIMIT_BYTES)


def _rmsnorm_kernel(x_ref, g_ref, o_ref):
    x = x_ref[...]
    ms = jnp.mean(x * x, axis=-1, keepdims=True)
    o_ref[...] = (x * lax.rsqrt(ms + RMS_EPS) * g_ref[...]).astype(o_ref.dtype)


def rmsnorm(x, g, out_dtype):
    m, d = x.shape
    tm = min(m, 512)
    assert m % tm == 0
    return pl.pallas_call(
        _rmsnorm_kernel,
        grid=(m // tm,),
        in_specs=[pl.BlockSpec((tm, d), lambda i: (i, 0)),
                  pl.BlockSpec((1, d), lambda i: (0, 0))],
        out_specs=pl.BlockSpec((tm, d), lambda i: (i, 0)),
        out_shape=jax.ShapeDtypeStruct((m, d), out_dtype),
        compiler_params=_params("parallel"),
        name="rmsnorm",
    )(x, g.reshape(1, d))


def _mm_ws_kernel(*refs, epilogue, out_scale, joint):
    if joint:
        a_ref, a2_ref, w_ref, o_ref, o2_ref, wb_ref = refs
    else:
        a_ref, w_ref, o_ref, wb_ref = refs
    i = pl.program_id(1)

    @pl.when(i == 0)
    def _():
        wb_ref[...] = w_ref[...].astype(BF16)

    def compute(a):
        acc = jnp.dot(a, wb_ref[...], preferred_element_type=F32)
        if epilogue == "relu2":
            h = jnp.maximum(acc, 0.0)
            acc = h * h
        if out_scale is not None:
            acc = acc * out_scale
        return acc

    if not joint:
        o_ref[...] = compute(a_ref[...]).astype(o_ref.dtype)
    else:
        @pl.when(i == 0)
        def _():
            o2_ref[...] = compute(a2_ref[...]).astype(o2_ref.dtype)

        @pl.when(i > 0)
        def _():
            o_ref[...] = compute(a_ref[...]).astype(o_ref.dtype)


def matmul_ws(a, w, layer, epilogue, *, a2=None, col0=0, n=None, out_scale=None):
    m, kdim = a.shape
    n = w.shape[2] if n is None else n
    tm = min(m, 1024)
    tn = min(n, 1024)
    assert m % tm == 0 and n % tn == 0 and col0 % tn == 0 and w.shape[1] == kdim
    jb = col0 // tn
    nm = m // tm
    joint = a2 is not None
    out_dtype = F32 if epilogue == "f32" else BF16
    first = int(joint)
    in_specs = [pl.BlockSpec((tm, kdim), lambda j, i: (jnp.maximum(i - first, 0), 0))]
    out_specs = [pl.BlockSpec((tm, tn), lambda j, i: (jnp.maximum(i - first, 0), j))]
    out_shape = [jax.ShapeDtypeStruct((m, n), out_dtype)]
    args = [a]
    if joint:
        m2 = a2.shape[0]
        in_specs.append(pl.BlockSpec((m2, kdim), lambda j, i: (0, 0)))
        out_specs.append(pl.BlockSpec((m2, tn), lambda j, i: (0, j)))
        out_shape.append(jax.ShapeDtypeStruct((m2, n), out_dtype))
        args.append(a2)
    in_specs.append(pl.BlockSpec((None, kdim, tn), lambda j, i: (layer, 0, jb + j)))
    args.append(w)
    outs = pl.pallas_call(
        functools.partial(_mm_ws_kernel, epilogue=epilogue, out_scale=out_scale, joint=joint),
        grid=(n // tn, nm + first),
        in_specs=in_specs,
        out_specs=out_specs,
        out_shape=out_shape,
        scratch_shapes=[pltpu.VMEM((kdim, tn), BF16)],
        compiler_params=_params("parallel", "arbitrary"),
        name="matmul_ws_" + epilogue,
    )(*args)
    return tuple(outs) if joint else outs[0]


def _mm_res_kernel(*refs, nk, emit_x, emit_norm):
    a_ref, w_ref, r_ref = refs[:3]
    pos = 3
    g_ref = None
    if emit_norm:
        g_ref = refs[pos]
        pos += 1
    x_ref = hn_ref = None
    if emit_x:
        x_ref = refs[pos]
        pos += 1
    if emit_norm:
        hn_ref = refs[pos]
        pos += 1
    scratch = refs[pos:]

    def finish(acc):
        x = r_ref[...] + acc
        if emit_x:
            x_ref[...] = x
        if emit_norm:
            ms = jnp.mean(x * x, axis=-1, keepdims=True)
            hn_ref[...] = (x * lax.rsqrt(ms + RMS_EPS) * g_ref[...]).astype(hn_ref.dtype)

    part = jnp.dot(a_ref[...], w_ref[...], preferred_element_type=F32)
    if nk == 1:
        finish(part)
    else:
        acc_ref = scratch[0]
        k = pl.program_id(1)

        @pl.when(k == 0)
        def _():
            acc_ref[...] = part

        @pl.when(jnp.logical_and(k > 0, k < nk - 1))
        def _():
            acc_ref[...] += part

        @pl.when(k == nk - 1)
        def _():
            finish(acc_ref[...] + part)


def matmul_res(a, w, layer, res, *, gain=None, norm_dtype=BF16, emit_x=True):
    m, kdim = a.shape
    d = w.shape[2]
    tm = min(m, 512)
    tk = min(kdim, 2048)
    assert m % tm == 0 and kdim % tk == 0 and res.shape == (m, d)
    nk = kdim // tk
    emit_norm = gain is not None
    assert emit_x or emit_norm

    def k_block(i, k):
        return jnp.where(i % 2 == 0, k, nk - 1 - k)

    row_spec = pl.BlockSpec((tm, d), lambda i, k: (i, 0))
    in_specs = [pl.BlockSpec((tm, tk), lambda i, k: (i, k_block(i, k))),
                pl.BlockSpec((None, tk, d), lambda i, k: (layer, k_block(i, k), 0)),
                row_spec]
    args = [a, w, res]
    out_shape, out_specs = [], []
    if emit_norm:
        in_specs.append(pl.BlockSpec((1, d), lambda i, k: (0, 0)))
        args.append(gain.reshape(1, d))
    if emit_x:
        out_shape.append(jax.ShapeDtypeStruct((m, d), F32))
        out_specs.append(row_spec)
    if emit_norm:
        out_shape.append(jax.ShapeDtypeStruct((m, d), norm_dtype))
        out_specs.append(row_spec)
    outs = pl.pallas_call(
        functools.partial(_mm_res_kernel, nk=nk, emit_x=emit_x, emit_norm=emit_norm),
        grid=(m // tm, nk),
        in_specs=in_specs,
        out_specs=out_specs,
        out_shape=out_shape,
        scratch_shapes=[pltpu.VMEM((tm, d), F32)] if nk > 1 else [],
        compiler_params=_params("parallel", "arbitrary"),
        name="matmul_res",
    )(*args)
    outs = list(outs)
    x_new = outs.pop(0) if emit_x else None
    hn = outs.pop(0) if emit_norm else None
    return x_new, hn


def _gate_kernel(u_ref, v_ref, g_ref, w_ref, b_ref, p_ref, *z_refs, L, sub, nchunk):
    v = v_ref[...]
    ms = jnp.mean(v * v, axis=-1, keepdims=True)
    z = v * lax.rsqrt(ms + RMS_EPS) * g_ref[...]
    if z_refs:
        z_refs[0][...] = z
    zb = z.astype(BF16)
    gdim = v.shape[1] // SGU_GROUPS
    row = lax.broadcasted_iota(jnp.int32, (L, L), 0)
    col = lax.broadcasted_iota(jnp.int32, (L, L), 1)
    mask = jnp.logical_and(row // sub == col // sub, col % sub <= row % sub)
    for g in range(SGU_GROUPS):
        wg = jnp.where(mask, w_ref[g], 0.0).astype(BF16)
        cs = slice(g * gdim, (g + 1) * gdim)
        for c in range(nchunk):
            rs = slice(c * L, (c + 1) * L)
            s = jnp.dot(wg, zb[rs, cs], preferred_element_type=F32) + b_ref[:, cs]
            p_ref[rs, cs] = (u_ref[rs, cs] * s).astype(BF16)


def spatial_gate(uv, g_v, w_tile, b_full, *, sub, emit_z):
    m, e2 = uv.shape
    e = e2 // 2
    L = w_tile.shape[1]
    rows = min(m, 512)
    assert m % rows == 0 and rows % L == 0
    nchunk = rows // L
    out_shape = [jax.ShapeDtypeStruct((m, e), BF16)]
    out_specs = [pl.BlockSpec((rows, e), lambda i: (i, 0))]
    if emit_z:
        out_shape.append(jax.ShapeDtypeStruct((m, e), F32))
        out_specs.append(pl.BlockSpec((rows, e), lambda i: (i, 0)))
    outs = pl.pallas_call(
        functools.partial(_gate_kernel, L=L, sub=sub, nchunk=nchunk),
        grid=(m // rows,),
        in_specs=[pl.BlockSpec((rows, e), lambda i: (i, 0)),
                  pl.BlockSpec((rows, e), lambda i: (i, 1)),
                  pl.BlockSpec((1, e), lambda i: (0, 0)),
                  pl.BlockSpec(w_tile.shape, lambda i: (0, 0, 0)),
                  pl.BlockSpec((L, e), lambda i: (0, 0))],
        out_specs=out_specs,
        out_shape=out_shape,
        compiler_params=_params("parallel"),
        name="spatial_gate",
    )(uv, uv, g_v.reshape(1, e), w_tile, b_full)
    return (outs[0], outs[1]) if emit_z else (outs[0], None)


def _sb_core(z, mask, c, tri, key_axis, split):
    lr = -(jnp.maximum(z, 0.0) + jnp.log2(1.0 + jnp.exp2(-jnp.abs(z))))
    if mask is not None:
        lr = jnp.where(mask, lr, 0.0)
    lrb = lr.astype(BF16)
    if split:
        lrb = jnp.concatenate([lrb, (lr - lrb.astype(F32)).astype(BF16)], axis=key_axis)
    if key_axis == 1:
        incl = jnp.dot(lrb, tri, preferred_element_type=F32)
        total = incl[:, 0:1]
    else:
        incl = jnp.dot(tri, lrb, preferred_element_type=F32)
        total = incl[0:1, :]
    a = jnp.exp2(z + incl + c)
    if mask is not None:
        a = jnp.where(mask, a, 0.0)
    return a, c + total


def _suffix_ones(tk, key_axis, copies):
    shape = (copies * tk, tk) if key_axis == 1 else (tk, copies * tk)
    j = lax.broadcasted_iota(jnp.int32, shape, 1 - key_axis) % tk
    s = lax.broadcasted_iota(jnp.int32, shape, key_axis)
    return (j >= s).astype(BF16)


_NT = (((1,), (1,)), ((), ()))


def _attn_prompt_kernel(bias_ref, q_ref, k_ref, v_ref, o_ref, kb_ref, vb_ref, uu_ref, *, tq, tk, hps):
    hp = pl.program_id(1)
    qs = pl.program_id(2)
    nsub = tq // tk
    heads = [slice(s * HEAD_DIM, (s + 1) * HEAD_DIM) for s in range(hps)]

    @pl.when(qs == 0)
    def _():
        kb_ref[...] = k_ref[...].astype(BF16)
        vb_ref[...] = v_ref[...].astype(BF16)
        uu_ref[...] = _suffix_ones(tk, 1, 1)

    q = [q_ref[:, hs] for hs in heads]
    bias = [bias_ref[hp * hps + s] * LOG2E for s in range(hps)]
    row = lax.broadcasted_iota(jnp.int32, (tq, tk), 0)
    col = lax.broadcasted_iota(jnp.int32, (tq, tk), 1)

    def block(kj, diag, carry):
        start = pl.multiple_of(kj * tk, tk)
        mask = None if diag is None else (col + diag * tk < row)
        out = []
        for s, hs in enumerate(heads):
            c, acc = carry[s]
            z = lax.dot_general(q[s], kb_ref[pl.ds(start, tk), hs], _NT,
                                preferred_element_type=F32) + bias[s]
            a, c = _sb_core(z, mask, c, uu_ref[...], 1, split=False)
            acc = acc + jnp.dot(a.astype(BF16), vb_ref[pl.ds(start, tk), hs],
                                preferred_element_type=F32)
            out.append((c, acc))
        return tuple(out)

    def group(first, diag, carry):
        for d in reversed(range(nsub)):
            carry = block(first + d, d if diag else None, carry)
        return carry

    zero = (jnp.zeros((tq, 1), F32), jnp.zeros((tq, HEAD_DIM), F32))
    carry = group(qs * nsub, True, (zero,) * hps)
    carry = lax.fori_loop(0, qs, lambda it, ca: group((qs - 1 - it) * nsub, False, ca), carry)
    for s, hs in enumerate(heads):
        o_ref[:, hs] = carry[s][1].astype(o_ref.dtype)


def attention_prompt(q, k, v, bias, *, tq=512, tk=256, hps=2):
    b, t, _ = q.shape
    tq = min(tq, t)
    tk = min(tk, tq)
    assert t % tq == 0 and tq % tk == 0 and N_HEADS % hps == 0
    w = hps * HEAD_DIM
    return pl.pallas_call(
        functools.partial(_attn_prompt_kernel, tq=tq, tk=tk, hps=hps),
        grid=(b, N_HEADS // hps, t // tq),
        in_specs=[pl.BlockSpec(memory_space=pltpu.SMEM),
                  pl.BlockSpec((None, tq, w), lambda bi, h, qi: (bi, qi, h)),
                  pl.BlockSpec((None, t, w), lambda bi, h, qi: (bi, 0, h)),
                  pl.BlockSpec((None, t, w), lambda bi, h, qi: (bi, 0, h))],
        out_specs=pl.BlockSpec((None, tq, w), lambda bi, h, qi: (bi, qi, h)),
        out_shape=jax.ShapeDtypeStruct(q.shape, BF16),
        scratch_shapes=[pltpu.VMEM((t, w), BF16),
                        pltpu.VMEM((t, w), BF16),
                        pltpu.VMEM((tk, tk), BF16)],
        compiler_params=_params("parallel", "parallel", "arbitrary"),
        name="attention_prompt",
    )(bias, q, k, v)


def _attn_sample_kernel(pt_ref, qt_ref, kn_ref, vn_ref, bias_ref, ck_hbm, cv_hbm, o_ref,
                        qbd_ref, acc_ref, c_ref, ll_ref, kb_ref, kbuf, vbuf, sems,
                        *, tq, page, ppstep, n_pages):
    bi = pl.program_id(0)
    j = pl.program_id(1)
    nsteps = pl.num_programs(1)
    cols = N_HEADS * tq
    d = N_HEADS * HEAD_DIM

    def page_copies(step, slot):
        copies = []
        for i in range(ppstep):
            pidx = pt_ref[bi, n_pages - 1 - ((step - 1) * ppstep + i)]
            for t, (hbm, buf) in enumerate(((ck_hbm, kbuf), (cv_hbm, vbuf))):
                copies.append(pltpu.make_async_copy(
                    hbm.at[pidx], buf.at[slot, i, :, pl.ds(0, N_HEADS), :], sems.at[slot, t, i]))
        return copies

    @pl.when(j == 0)
    def _():
        for step in range(1, PAGE_SLOTS):
            @pl.when(step < nsteps)
            def _():
                for cp in page_copies(step, step % PAGE_SLOTS):
                    cp.start()
        rh = lax.broadcasted_iota(jnp.int32, (d, cols), 0) // HEAD_DIM
        ch = lax.broadcasted_iota(jnp.int32, (d, cols), 1) // tq
        qbd_ref[...] = jnp.where(rh == ch, qt_ref[...], 0.0).astype(BF16)
        acc_ref[...] = jnp.zeros_like(acc_ref)
        c_ref[...] = jnp.zeros_like(c_ref)
        ll_ref[...] = _suffix_ones(page, 0, 2)

    def attend(kb, v_head, n, masked):
        z_all = jnp.dot(kb, qbd_ref[...], preferred_element_type=F32)
        mask = None
        if masked:
            s_idx = lax.broadcasted_iota(jnp.int32, (page, cols), 0)
            t_idx = lax.broadcasted_iota(jnp.int32, (page, cols), 1) % tq
            mask = s_idx < t_idx
        c = c_ref[...]
        p_parts = []
        for i in range(n):
            z = z_all[i * page:(i + 1) * page] + bias_ref[...] * LOG2E
            a, c = _sb_core(z, mask, c, ll_ref[...], 0, split=True)
            p_parts.append(a.T)
        c_ref[...] = c
        p = p_parts[0] if n == 1 else jnp.concatenate(p_parts, axis=1)
        for h in range(N_HEADS):
            rs = slice(h * tq, (h + 1) * tq)
            acc_ref[rs, :] += jnp.dot(p[rs].astype(BF16), v_head(h), preferred_element_type=F32)

    @pl.when(j == 0)
    def _():
        attend(kn_ref[...].astype(BF16),
               lambda h: vn_ref[:, h * HEAD_DIM:(h + 1) * HEAD_DIM].astype(BF16), 1, True)

    @pl.when(j > 0)
    def _():
        slot = j % PAGE_SLOTS
        ahead = j + PAGE_SLOTS - 1

        @pl.when(ahead < nsteps)
        def _():
            for cp in page_copies(ahead, ahead % PAGE_SLOTS):
                cp.start()

        for cp in page_copies(j, slot):
            cp.wait()

        rows_per_page = page * PAGE_PITCH
        kflat = kbuf.reshape(PAGE_SLOTS * ppstep * rows_per_page, HEAD_DIM)
        vflat = vbuf.reshape(PAGE_SLOTS * ppstep * rows_per_page, HEAD_DIM)

        def head_rows(flat, i, h):
            start = (slot * ppstep + i) * rows_per_page + h
            return flat[pl.ds(start, page, stride=PAGE_PITCH), :].astype(BF16)

        for i in range(ppstep):
            for h in range(N_HEADS):
                kb_ref[i * page:(i + 1) * page, h * HEAD_DIM:(h + 1) * HEAD_DIM] = head_rows(kflat, i, h)

        def v_head(h):
            parts = [head_rows(vflat, i, h) for i in range(ppstep)]
            return parts[0] if ppstep == 1 else jnp.concatenate(parts, axis=0)

        attend(kb_ref[...], v_head, ppstep, False)

    @pl.when(j == pl.num_programs(1) - 1)
    def _():
        for h in range(N_HEADS):
            o_ref[:, h * HEAD_DIM:(h + 1) * HEAD_DIM] = acc_ref[h * tq:(h + 1) * tq, :]


def attention_sample(q, k_new, v_new, cache_k, cache_v, page_table, bias, *, ppstep=4):
    db, tq, d = q.shape
    page = cache_k.shape[1]
    n_pages = page_table.shape[1]
    ppstep = min(ppstep, n_pages)
    assert n_pages % ppstep == 0
    cols = N_HEADS * tq
    pad = ((0, 0), (0, page - tq), (0, 0))
    kn = jnp.pad(k_new, pad)
    vn = jnp.pad(v_new, pad)
    qt = jnp.tile(jnp.swapaxes(q, 1, 2), (1, 1, N_HEADS)).astype(F32)
    bias_row = jnp.repeat(bias, tq).reshape(1, cols)

    def seq_block(bi, j, pt):
        return (bi, 0, 0)

    page_buf = pltpu.VMEM((PAGE_SLOTS, ppstep, page, PAGE_PITCH, HEAD_DIM), F32)
    grid_spec = pltpu.PrefetchScalarGridSpec(
        num_scalar_prefetch=1,
        grid=(db, n_pages // ppstep + 1),
        in_specs=[pl.BlockSpec((None, d, cols), seq_block),
                  pl.BlockSpec((None, page, d), seq_block),
                  pl.BlockSpec((None, page, d), seq_block),
                  pl.BlockSpec((1, cols), lambda bi, j, pt: (0, 0)),
                  pl.BlockSpec(memory_space=pl.ANY),
                  pl.BlockSpec(memory_space=pl.ANY)],
        out_specs=pl.BlockSpec((None, tq, d), seq_block),
        scratch_shapes=[pltpu.VMEM((d, cols), BF16),
                        pltpu.VMEM((cols, HEAD_DIM), F32),
                        pltpu.VMEM((1, cols), F32),
                        pltpu.VMEM((page, 2 * page), BF16),
                        pltpu.VMEM((ppstep * page, d), BF16),
                        page_buf, page_buf,
                        pltpu.SemaphoreType.DMA((PAGE_SLOTS, 2, ppstep))],
    )
    return pl.pallas_call(
        functools.partial(_attn_sample_kernel, tq=tq, page=page, ppstep=ppstep, n_pages=n_pages),
        grid_spec=grid_spec,
        out_shape=jax.ShapeDtypeStruct((db, tq, d), F32),
        compiler_params=_params("arbitrary", "arbitrary"),
        name="attention_sample",
    )(page_table, qt, kn, vn, bias_row, cache_k, cache_v)


def _trunk(x3s, past, wts):
    dims = [x3.shape for x3 in x3s]
    d = dims[0][2]
    ms = [bb * tt for bb, tt, _ in dims]
    x = [x3.reshape(mm, d) for x3, mm in zip(x3s, ms)]
    n_a = wts["a_w_in"].shape[0]
    depth = wts["w_ffn_up"].shape[0]
    groups = range(2)
    hd = N_HEADS * HEAD_DIM

    def unzip(pairs):
        return [p[0] for p in pairs], [p[1] for p in pairs]

    def ffn(x, h, layer):
        hh = matmul_ws(h[0], wts["w_ffn_up"], layer, "relu2", a2=h[1])
        if layer + 1 < depth:
            kw = dict(gain=wts["norm_mix"][layer + 1])
        else:
            kw = dict(gain=wts["norm_final"], norm_dtype=F32, emit_x=False)
        return unzip([matmul_res(hh[g], wts["w_ffn_down"], layer, x[g], **kw) for g in groups])

    def gate(uv, layer, g):
        t, m = dims[g][1], ms[g]
        sub = min(t, CHUNK)
        L = sub if sub == CHUNK else min(m, 64)
        reps = L // sub
        w_tile = jnp.tile(wts["a_w_spatial"][layer][:, :sub, :sub], (1, reps, reps))
        b_rows = jnp.tile(wts["a_b_spatial"][layer][:, :sub].T, (reps, 1))
        b_full = jnp.repeat(b_rows, d // SGU_GROUPS, axis=1)
        return spatial_gate(uv, wts["a_norm_v"][layer], w_tile, b_full, sub=sub, emit_z=g == 1)

    z_states = []
    h = [rmsnorm(x[g], wts["norm_mix"][0], BF16) for g in groups]
    for layer in range(n_a):
        uv = matmul_ws(h[0], wts["a_w_in"], layer, "f32", a2=h[1])
        p, z = unzip([gate(uv[g], layer, g) for g in groups])
        z_states.append(z[1])
        x, h = unzip([matmul_res(p[g], wts["a_w_out"], layer, x[g], gain=wts["norm_ffn"][layer])
                      for g in groups])
        x, h = ffn(x, h, layer)

    h_kv = [rmsnorm(x[g], wts["kv_norm"], BF16) for g in groups]
    k_new = matmul_ws(h_kv[0], wts["w_kv"][None], 0, "f32", a2=h_kv[1], col0=0, n=hd)
    v_new = matmul_ws(h_kv[0], wts["w_kv"][None], 0, "f32", a2=h_kv[1], col0=hd, n=hd)

    q_scale = LOG2E * HEAD_DIM ** -0.5
    cache_k, cache_v, page_table = past
    for layer in range(n_a, depth):
        j = layer - n_a
        q = matmul_ws(h[0], wts["b_w_q"], j, "bf16", a2=h[1], out_scale=q_scale)
        (b0, t0, _), (b1, t1, _) = dims
        o0 = attention_prompt(q[0].reshape(b0, t0, hd), k_new[0].reshape(b0, t0, hd),
                              v_new[0].reshape(b0, t0, hd), wts["b_logit_bias"][j])
        o1 = attention_sample(q[1].reshape(b1, t1, hd), k_new[1].reshape(b1, t1, hd),
                              v_new[1].reshape(b1, t1, hd), cache_k, cache_v, page_table,
                              wts["b_logit_bias"][j])
        o = [o0.reshape(ms[0], hd), o1.reshape(ms[1], hd).astype(BF16)]
        x, h = unzip([matmul_res(o[g], wts["b_w_out"], j, x[g], gain=wts["norm_ffn"][layer])
                      for g in groups])
        x, h = ffn(x, h, layer)

    y = [h[g].reshape(dims[g]) for g in groups]
    k_out = [k_new[g].reshape(dims[g][0], dims[g][1], N_HEADS, HEAD_DIM) for g in groups]
    v_out = [v_new[g].reshape(dims[g][0], dims[g][1], N_HEADS, HEAD_DIM) for g in groups]
    return y, k_out, v_out, z_states


def kernel(x_prompt, x_sample, cache_k, cache_v, page_table, norm_mix, norm_ffn,
           w_ffn_up, w_ffn_down, a_w_in, a_norm_v, a_w_spatial, a_b_spatial, a_w_out,
           kv_norm, w_kv, b_w_q, b_logit_bias, b_w_out, norm_final):
    wts = dict(
        norm_mix=norm_mix, norm_ffn=norm_ffn,
        w_ffn_up=w_ffn_up, w_ffn_down=w_ffn_down.astype(BF16),
        a_w_in=a_w_in, a_norm_v=a_norm_v, a_w_spatial=a_w_spatial,
        a_b_spatial=a_b_spatial, a_w_out=a_w_out.astype(BF16), kv_norm=kv_norm,
        w_kv=w_kv, b_w_q=b_w_q, b_logit_bias=b_logit_bias,
        b_w_out=b_w_out.astype(BF16), norm_final=norm_final)

    y, k_out, v_out, z_states = _trunk([x_prompt, x_sample], (cache_k, cache_v, page_table), wts)
    db, ts, d = x_sample.shape
    sgu_v_sample = jnp.stack(z_states).reshape(len(z_states), db, ts, d)
    return (y[0], y[1], k_out[0], v_out[0], k_out[1], v_out[1], sgu_v_sample)
```

```python
import functools

import jax
import jax.numpy as jnp
from jax import lax
from jax.experimental import pallas as pl
from jax.experimental.pallas import tpu as pltpu

N_HEADS = 16
HEAD_DIM = 128
CHUNK = 128
SGU_GROUPS = 8
RMS_EPS = 1e-6
LOG2E = 1.4426950408889634
PAGE_PITCH = 24
PAGE_SLOTS = 3

VMEM_LIMIT_BYTES = 56 * 1024 * 1024

F32 = jnp.float32
BF16 = jnp.bfloat16


def _params(*semantics):
    return pltpu.CompilerParams(dimension_semantics=semantics,
                                vmem_limit_bytes=VMEM_LIMIT_BYTES)


def _rmsnorm_kernel(x_ref, g_ref, o_ref):
    x = x_ref[...]
    ms = jnp.mean(x * x, axis=-1, keepdims=True)
    o_ref[...] = (x * lax.rsqrt(ms + RMS_EPS) * g_ref[...]).astype(o_ref.dtype)


def rmsnorm(x, g, out_dtype):
    m, d = x.shape
    tm = min(m, 512)
    assert m % tm == 0
    return pl.pallas_call(
        _rmsnorm_kernel,
        grid=(m // tm,),
        in_specs=[pl.BlockSpec((tm, d), lambda i: (i, 0)),
                  pl.BlockSpec((1, d), lambda i: (0, 0))],
        out_specs=pl.BlockSpec((tm, d), lambda i: (i, 0)),
        out_shape=jax.ShapeDtypeStruct((m, d), out_dtype),
        compiler_params=_params("parallel"),
        name="rmsnorm",
    )(x, g.reshape(1, d))


def _mm_ws_kernel(*refs, epilogue, out_scale, joint, bf16_copy):
    refs = list(refs)
    a_ref = refs.pop(0)
    a2_ref = refs.pop(0) if joint else None
    w_ref = refs.pop(0)
    o_ref = refs.pop(0)
    o2_ref = refs.pop(0) if joint else None
    ob_ref = refs.pop(0) if bf16_copy else None
    (wb_ref,) = refs
    i = pl.program_id(1)

    @pl.when(i == 0)
    def _():
        wb_ref[...] = w_ref[...].astype(BF16)

    def compute(a):
        acc = jnp.dot(a, wb_ref[...], preferred_element_type=F32)
        if epilogue == "relu2":
            h = jnp.maximum(acc, 0.0)
            acc = h * h
        if out_scale is not None:
            acc = acc * out_scale
        return acc

    def main():
        acc = compute(a_ref[...])
        o_ref[...] = acc.astype(o_ref.dtype)
        if bf16_copy:
            ob_ref[...] = acc.astype(BF16)

    if not joint:
        main()
    else:
        @pl.when(i == 0)
        def _():
            o2_ref[...] = compute(a2_ref[...]).astype(o2_ref.dtype)

        pl.when(i > 0)(main)


def matmul_ws(a, w, layer, epilogue, *, a2=None, col0=0, n=None, out_scale=None, bf16_copy=False):
    m, kdim = a.shape
    n = w.shape[2] if n is None else n
    tm = min(m, 1024)
    tn = min(n, 1024)
    assert m % tm == 0 and n % tn == 0 and col0 % tn == 0 and w.shape[1] == kdim
    jb = col0 // tn
    nm = m // tm
    joint = a2 is not None
    out_dtype = F32 if epilogue == "f32" else BF16
    first = int(joint)
    main_spec = pl.BlockSpec((tm, tn), lambda j, i: (jnp.maximum(i - first, 0), j))
    in_specs = [pl.BlockSpec((tm, kdim), lambda j, i: (jnp.maximum(i - first, 0), 0))]
    out_specs = [main_spec]
    out_shape = [jax.ShapeDtypeStruct((m, n), out_dtype)]
    args = [a]
    if joint:
        m2 = a2.shape[0]
        in_specs.append(pl.BlockSpec((m2, kdim), lambda j, i: (0, 0)))
        out_specs.append(pl.BlockSpec((m2, tn), lambda j, i: (0, j)))
        out_shape.append(jax.ShapeDtypeStruct((m2, n), out_dtype))
        args.append(a2)
    if bf16_copy:
        out_specs.append(main_spec)
        out_shape.append(jax.ShapeDtypeStruct((m, n), BF16))
    in_specs.append(pl.BlockSpec((None, kdim, tn), lambda j, i: (layer, 0, jb + j)))
    args.append(w)
    outs = pl.pallas_call(
        functools.partial(_mm_ws_kernel, epilogue=epilogue, out_scale=out_scale, joint=joint,
                          bf16_copy=bf16_copy),
        grid=(n // tn, nm + first),
        in_specs=in_specs,
        out_specs=out_specs,
        out_shape=out_shape,
        scratch_shapes=[pltpu.VMEM((kdim, tn), BF16)],
        compiler_params=_params("parallel", "arbitrary"),
        name="matmul_ws_" + epilogue,
    )(*args)
    return tuple(outs) if len(outs) > 1 else outs[0]


def _mm_res_kernel(*refs, nk, emit_x, emit_norm):
    a_ref, w_ref, r_ref = refs[:3]
    pos = 3
    g_ref = None
    if emit_norm:
        g_ref = refs[pos]
        pos += 1
    x_ref = hn_ref = None
    if emit_x:
        x_ref = refs[pos]
        pos += 1
    if emit_norm:
        hn_ref = refs[pos]
        pos += 1
    scratch = refs[pos:]

    def finish(acc):
        x = r_ref[...] + acc
        if emit_x:
            x_ref[...] = x
        if emit_norm:
            ms = jnp.mean(x * x, axis=-1, keepdims=True)
            hn_ref[...] = (x * lax.rsqrt(ms + RMS_EPS) * g_ref[...]).astype(hn_ref.dtype)

    def part():
        return jnp.dot(a_ref[...], w_ref[...], preferred_element_type=F32)

    if nk == 1:
        finish(part())
    else:
        acc_ref = scratch[0]
        k = pl.program_id(1)

        @pl.when(k == 0)
        def _():
            acc_ref[...] = part()

        @pl.when(jnp.logical_and(k > 0, k < nk - 1))
        def _():
            acc_ref[...] += part()

        @pl.when(k == nk - 1)
        def _():
            finish(acc_ref[...] + part())


def matmul_res(a, w, layer, res, *, gain=None, norm_dtype=BF16, emit_x=True):
    m, kdim = a.shape
    d = w.shape[2]
    tm = min(m, 512)
    tk = min(kdim, 2048)
    assert m % tm == 0 and kdim % tk == 0 and res.shape == (m, d)
    nk = kdim // tk
    emit_norm = gain is not None
    assert emit_x or emit_norm

    def k_block(i, k):
        return jnp.where(i % 2 == 0, k, nk - 1 - k)

    row_spec = pl.BlockSpec((tm, d), lambda i, k: (i, 0))
    in_specs = [pl.BlockSpec((tm, tk), lambda i, k: (i, k_block(i, k))),
                pl.BlockSpec((None, tk, d), lambda i, k: (layer, k_block(i, k), 0)),
                row_spec]
    args = [a, w, res]
    out_shape, out_specs = [], []
    if emit_norm:
        in_specs.append(pl.BlockSpec((1, d), lambda i, k: (0, 0)))
        args.append(gain.reshape(1, d))
    if emit_x:
        out_shape.append(jax.ShapeDtypeStruct((m, d), F32))
        out_specs.append(row_spec)
    if emit_norm:
        out_shape.append(jax.ShapeDtypeStruct((m, d), norm_dtype))
        out_specs.append(row_spec)
    outs = pl.pallas_call(
        functools.partial(_mm_res_kernel, nk=nk, emit_x=emit_x, emit_norm=emit_norm),
        grid=(m // tm, nk),
        in_specs=in_specs,
        out_specs=out_specs,
        out_shape=out_shape,
        scratch_shapes=[pltpu.VMEM((tm, d), F32)] if nk > 1 else [],
        compiler_params=_params("parallel", "arbitrary"),
        name="matmul_res",
    )(*args)
    outs = list(outs)
    x_new = outs.pop(0) if emit_x else None
    hn = outs.pop(0) if emit_norm else None
    return x_new, hn


def _gate_kernel(u_ref, v_ref, g_ref, w_ref, b_ref, p_ref, *z_refs, L, sub, nchunk):
    v = v_ref[...]
    ms = jnp.mean(v * v, axis=-1, keepdims=True)
    z = v * lax.rsqrt(ms + RMS_EPS) * g_ref[...]
    if z_refs:
        z_refs[0][...] = z
    zb = z.astype(BF16)
    gdim = v.shape[1] // SGU_GROUPS
    row = lax.broadcasted_iota(jnp.int32, (L, L), 0)
    col = lax.broadcasted_iota(jnp.int32, (L, L), 1)
    mask = jnp.logical_and(row // sub == col // sub, col % sub <= row % sub)
    for g in range(SGU_GROUPS):
        wg = jnp.where(mask, w_ref[g], 0.0).astype(BF16)
        cs = slice(g * gdim, (g + 1) * gdim)
        for c in range(nchunk):
            rs = slice(c * L, (c + 1) * L)
            s = jnp.dot(wg, zb[rs, cs], preferred_element_type=F32) + b_ref[:, cs]
            p_ref[rs, cs] = (u_ref[rs, cs] * s).astype(BF16)


def spatial_gate(uv, g_v, w_tile, b_full, *, sub, emit_z):
    m, e2 = uv.shape
    e = e2 // 2
    L = w_tile.shape[1]
    rows = min(m, 512)
    assert m % rows == 0 and rows % L == 0
    nchunk = rows // L
    out_shape = [jax.ShapeDtypeStruct((m, e), BF16)]
    out_specs = [pl.BlockSpec((rows, e), lambda i: (i, 0))]
    if emit_z:
        out_shape.append(jax.ShapeDtypeStruct((m, e), F32))
        out_specs.append(pl.BlockSpec((rows, e), lambda i: (i, 0)))
    outs = pl.pallas_call(
        functools.partial(_gate_kernel, L=L, sub=sub, nchunk=nchunk),
        grid=(m // rows,),
        in_specs=[pl.BlockSpec((rows, e), lambda i: (i, 0)),
                  pl.BlockSpec((rows, e), lambda i: (i, 1)),
                  pl.BlockSpec((1, e), lambda i: (0, 0)),
                  pl.BlockSpec(w_tile.shape, lambda i: (0, 0, 0)),
                  pl.BlockSpec((L, e), lambda i: (0, 0))],
        out_specs=out_specs,
        out_shape=out_shape,
        compiler_params=_params("parallel"),
        name="spatial_gate",
    )(uv, uv, g_v.reshape(1, e), w_tile, b_full)
    return (outs[0], outs[1]) if emit_z else (outs[0], None)


def _sb_core(z, mask, c, tri, key_axis, split):
    lr = -(jnp.maximum(z, 0.0) + jnp.log2(1.0 + jnp.exp2(-jnp.abs(z))))
    if mask is not None:
        lr = jnp.where(mask, lr, 0.0)
    lrb = lr.astype(BF16)
    if split:
        lrb = jnp.concatenate([lrb, (lr - lrb.astype(F32)).astype(BF16)], axis=key_axis)
    if key_axis == 1:
        incl = jnp.dot(lrb, tri, preferred_element_type=F32)
        total = incl[:, 0:1]
    else:
        incl = jnp.dot(tri, lrb, preferred_element_type=F32)
        total = incl[0:1, :]
    a = jnp.exp2(z + incl + c)
    if mask is not None:
        a = jnp.where(mask, a, 0.0)
    return a, c + total


def _suffix_ones(tk, key_axis, copies):
    shape = (copies * tk, tk) if key_axis == 1 else (tk, copies * tk)
    j = lax.broadcasted_iota(jnp.int32, shape, 1 - key_axis) % tk
    s = lax.broadcasted_iota(jnp.int32, shape, key_axis)
    return (j >= s).astype(BF16)


_NT = (((1,), (1,)), ((), ()))


def _attn_prompt_kernel(bias_ref, q_ref, kb_ref, vb_ref, o_ref, uu_ref, *, tq, tk, hps):
    hp = pl.program_id(1)
    qs = pl.program_id(2)
    nsub = tq // tk
    heads = [slice(s * HEAD_DIM, (s + 1) * HEAD_DIM) for s in range(hps)]

    @pl.when(qs == 0)
    def _():
        uu_ref[...] = _suffix_ones(tk, 1, 1)

    q = [q_ref[:, hs] for hs in heads]
    bias = [bias_ref[hp * hps + s] * LOG2E for s in range(hps)]
    row = lax.broadcasted_iota(jnp.int32, (tq, tk), 0)
    col = lax.broadcasted_iota(jnp.int32, (tq, tk), 1)

    def block(kj, diag, carry):
        start = pl.multiple_of(kj * tk, tk)
        mask = None if diag is None else (col + diag * tk < row)
        out = []
        for s, hs in enumerate(heads):
            c, acc = carry[s]
            z = lax.dot_general(q[s], kb_ref[pl.ds(start, tk), hs], _NT,
                                preferred_element_type=F32) + bias[s]
            a, c = _sb_core(z, mask, c, uu_ref[...], 1, split=False)
            acc = acc + jnp.dot(a.astype(BF16), vb_ref[pl.ds(start, tk), hs],
                                preferred_element_type=F32)
            out.append((c, acc))
        return tuple(out)

    def group(first, diag, carry):
        for d in reversed(range(nsub)):
            carry = block(first + d, d if diag else None, carry)
        return carry

    zero = (jnp.zeros((tq, 1), F32), jnp.zeros((tq, HEAD_DIM), F32))
    carry = group(qs * nsub, True, (zero,) * hps)
    carry = lax.fori_loop(0, qs, lambda it, ca: group((qs - 1 - it) * nsub, False, ca), carry)
    for s, hs in enumerate(heads):
        o_ref[:, hs] = carry[s][1].astype(o_ref.dtype)


def attention_prompt(q, k, v, bias, *, tq=512, tk=256, hps=2):
    b, t, _ = q.shape
    tq = min(tq, t)
    tk = min(tk, tq)
    assert t % tq == 0 and tq % tk == 0 and N_HEADS % hps == 0
    w = hps * HEAD_DIM
    return pl.pallas_call(
        functools.partial(_attn_prompt_kernel, tq=tq, tk=tk, hps=hps),
        grid=(b, N_HEADS // hps, t // tq),
        in_specs=[pl.BlockSpec(memory_space=pltpu.SMEM),
                  pl.BlockSpec((None, tq, w), lambda bi, h, qi: (bi, qi, h)),
                  pl.BlockSpec((None, t, w), lambda bi, h, qi: (bi, 0, h)),
                  pl.BlockSpec((None, t, w), lambda bi, h, qi: (bi, 0, h))],
        out_specs=pl.BlockSpec((None, tq, w), lambda bi, h, qi: (bi, qi, h)),
        out_shape=jax.ShapeDtypeStruct(q.shape, BF16),
        scratch_shapes=[pltpu.VMEM((tk, tk), BF16)],
        compiler_params=_params("parallel", "parallel", "arbitrary"),
        name="attention_prompt",
    )(bias, q, k, v)


def _attn_sample_kernel(pt_ref, qt_ref, kn_ref, vn_ref, bias_ref, ck_hbm, cv_hbm, o_ref,
                        qbd_ref, acc_ref, c_ref, ll_ref, kb_ref, kbuf, vbuf, sems,
                        *, tq, page, ppstep, n_pages):
    bi = pl.program_id(0)
    j = pl.program_id(1)
    nsteps = pl.num_programs(1)
    cols = N_HEADS * tq
    d = N_HEADS * HEAD_DIM

    def page_copies(step, slot):
        copies = []
        for i in range(ppstep):
            pidx = pt_ref[bi, n_pages - 1 - ((step - 1) * ppstep + i)]
            for t, (hbm, buf) in enumerate(((ck_hbm, kbuf), (cv_hbm, vbuf))):
                copies.append(pltpu.make_async_copy(
                    hbm.at[pidx], buf.at[slot, i, :, pl.ds(0, N_HEADS), :], sems.at[slot, t, i]))
        return copies

    @pl.when(j == 0)
    def _():
        for step in range(1, PAGE_SLOTS):
            @pl.when(step < nsteps)
            def _():
                for cp in page_copies(step, step % PAGE_SLOTS):
                    cp.start()
        rh = lax.broadcasted_iota(jnp.int32, (d, cols), 0) // HEAD_DIM
        ch = lax.broadcasted_iota(jnp.int32, (d, cols), 1) // tq
        qbd_ref[...] = jnp.where(rh == ch, qt_ref[...], 0.0).astype(BF16)
        acc_ref[...] = jnp.zeros_like(acc_ref)
        c_ref[...] = jnp.zeros_like(c_ref)
        ll_ref[...] = _suffix_ones(page, 0, 2)

    def attend(kb, v_head, n, masked):
        z_all = jnp.dot(kb, qbd_ref[...], preferred_element_type=F32)
        mask = None
        if masked:
            s_idx = lax.broadcasted_iota(jnp.int32, (page, cols), 0)
            t_idx = lax.broadcasted_iota(jnp.int32, (page, cols), 1) % tq
            mask = s_idx < t_idx
        c = c_ref[...]
        p_parts = []
        for i in range(n):
            z = z_all[i * page:(i + 1) * page] + bias_ref[...] * LOG2E
            a, c = _sb_core(z, mask, c, ll_ref[...], 0, split=True)
            p_parts.append(a.T)
        c_ref[...] = c
        p = p_parts[0] if n == 1 else jnp.concatenate(p_parts, axis=1)
        for h in range(N_HEADS):
            rs = slice(h * tq, (h + 1) * tq)
            acc_ref[rs, :] += jnp.dot(p[rs].astype(BF16), v_head(h), preferred_element_type=F32)

    @pl.when(j == 0)
    def _():
        attend(kn_ref[...].astype(BF16),
               lambda h: vn_ref[:, h * HEAD_DIM:(h + 1) * HEAD_DIM].astype(BF16), 1, True)

    @pl.when(j > 0)
    def _():
        slot = j % PAGE_SLOTS
        ahead = j + PAGE_SLOTS - 1

        @pl.when(ahead < nsteps)
        def _():
            for cp in page_copies(ahead, ahead % PAGE_SLOTS):
                cp.start()

        for cp in page_copies(j, slot):
            cp.wait()

        rows_per_page = page * PAGE_PITCH
        kflat = kbuf.reshape(PAGE_SLOTS * ppstep * rows_per_page, HEAD_DIM)
        vflat = vbuf.reshape(PAGE_SLOTS * ppstep * rows_per_page, HEAD_DIM)

        def head_rows(flat, i, h):
            start = (slot * ppstep + i) * rows_per_page + h
            return flat[pl.ds(start, page, stride=PAGE_PITCH), :].astype(BF16)

        for i in range(ppstep):
            for h in range(N_HEADS):
                kb_ref[i * page:(i + 1) * page, h * HEAD_DIM:(h + 1) * HEAD_DIM] = head_rows(kflat, i, h)

        def v_head(h):
            parts = [head_rows(vflat, i, h) for i in range(ppstep)]
            return parts[0] if ppstep == 1 else jnp.concatenate(parts, axis=0)

        attend(kb_ref[...], v_head, ppstep, False)

    @pl.when(j == pl.num_programs(1) - 1)
    def _():
        for h in range(N_HEADS):
            o_ref[:, h * HEAD_DIM:(h + 1) * HEAD_DIM] = acc_ref[h * tq:(h + 1) * tq, :]


def attention_sample(q, k_new, v_new, cache_k, cache_v, page_table, bias, *, ppstep=4):
    db, tq, d = q.shape
    page = cache_k.shape[1]
    n_pages = page_table.shape[1]
    ppstep = min(ppstep, n_pages)
    assert n_pages % ppstep == 0
    cols = N_HEADS * tq
    pad = ((0, 0), (0, page - tq), (0, 0))
    kn = jnp.pad(k_new, pad)
    vn = jnp.pad(v_new, pad)
    qt = jnp.tile(jnp.swapaxes(q, 1, 2), (1, 1, N_HEADS)).astype(F32)
    bias_row = jnp.repeat(bias, tq).reshape(1, cols)

    def seq_block(bi, j, pt):
        return (bi, 0, 0)

    page_buf = pltpu.VMEM((PAGE_SLOTS, ppstep, page, PAGE_PITCH, HEAD_DIM), F32)
    grid_spec = pltpu.PrefetchScalarGridSpec(
        num_scalar_prefetch=1,
        grid=(db, n_pages // ppstep + 1),
        in_specs=[pl.BlockSpec((None, d, cols), seq_block),
                  pl.BlockSpec((None, page, d), seq_block),
                  pl.BlockSpec((None, page, d), seq_block),
                  pl.BlockSpec((1, cols), lambda bi, j, pt: (0, 0)),
                  pl.BlockSpec(memory_space=pl.ANY),
                  pl.BlockSpec(memory_space=pl.ANY)],
        out_specs=pl.BlockSpec((None, tq, d), seq_block),
        scratch_shapes=[pltpu.VMEM((d, cols), BF16),
                        pltpu.VMEM((cols, HEAD_DIM), F32),
                        pltpu.VMEM((1, cols), F32),
                        pltpu.VMEM((page, 2 * page), BF16),
                        pltpu.VMEM((ppstep * page, d), BF16),
                        page_buf, page_buf,
                        pltpu.SemaphoreType.DMA((PAGE_SLOTS, 2, ppstep))],
    )
    return pl.pallas_call(
        functools.partial(_attn_sample_kernel, tq=tq, page=page, ppstep=ppstep, n_pages=n_pages),
        grid_spec=grid_spec,
        out_shape=jax.ShapeDtypeStruct((db, tq, d), F32),
        compiler_params=_params("arbitrary", "arbitrary"),
        name="attention_sample",
    )(page_table, qt, kn, vn, bias_row, cache_k, cache_v)


def _trunk(x3s, past, wts):
    dims = [x3.shape for x3 in x3s]
    d = dims[0][2]
    ms = [bb * tt for bb, tt, _ in dims]
    x = [x3.reshape(mm, d) for x3, mm in zip(x3s, ms)]
    n_a = wts["a_w_in"].shape[0]
    depth = wts["w_ffn_up"].shape[0]
    groups = range(2)
    hd = N_HEADS * HEAD_DIM

    def unzip(pairs):
        return [p[0] for p in pairs], [p[1] for p in pairs]

    def ffn(x, h, layer):
        hh = matmul_ws(h[0], wts["w_ffn_up"], layer, "relu2", a2=h[1])
        if layer + 1 < depth:
            kw = dict(gain=wts["norm_mix"][layer + 1])
        else:
            kw = dict(gain=wts["norm_final"], norm_dtype=F32, emit_x=False)
        return unzip([matmul_res(hh[g], wts["w_ffn_down"], layer, x[g], **kw) for g in groups])

    def gate(uv, layer, g):
        t, m = dims[g][1], ms[g]
        sub = min(t, CHUNK)
        L = sub if sub == CHUNK else min(m, 64)
        reps = L // sub
        w_tile = jnp.tile(wts["a_w_spatial"][layer][:, :sub, :sub], (1, reps, reps))
        b_rows = jnp.tile(wts["a_b_spatial"][layer][:, :sub].T, (reps, 1))
        b_full = jnp.repeat(b_rows, d // SGU_GROUPS, axis=1)
        return spatial_gate(uv, wts["a_norm_v"][layer], w_tile, b_full, sub=sub, emit_z=g == 1)

    z_states = []
    h = [rmsnorm(x[g], wts["norm_mix"][0], BF16) for g in groups]
    for layer in range(n_a):
        uv = matmul_ws(h[0], wts["a_w_in"], layer, "f32", a2=h[1])
        p, z = unzip([gate(uv[g], layer, g) for g in groups])
        z_states.append(z[1])
        x, h = unzip([matmul_res(p[g], wts["a_w_out"], layer, x[g], gain=wts["norm_ffn"][layer])
                      for g in groups])
        x, h = ffn(x, h, layer)

    h_kv = [rmsnorm(x[g], wts["kv_norm"], BF16) for g in groups]
    k_new = matmul_ws(h_kv[0], wts["w_kv"][None], 0, "f32", a2=h_kv[1], col0=0, n=hd, bf16_copy=True)
    v_new = matmul_ws(h_kv[0], wts["w_kv"][None], 0, "f32", a2=h_kv[1], col0=hd, n=hd, bf16_copy=True)

    q_scale = LOG2E * HEAD_DIM ** -0.5
    cache_k, cache_v, page_table = past
    for layer in range(n_a, depth):
        j = layer - n_a
        q = matmul_ws(h[0], wts["b_w_q"], j, "bf16", a2=h[1], out_scale=q_scale)
        (b0, t0, _), (b1, t1, _) = dims
        o0 = attention_prompt(q[0].reshape(b0, t0, hd), k_new[2].reshape(b0, t0, hd),
                              v_new[2].reshape(b0, t0, hd), wts["b_logit_bias"][j])
        o1 = attention_sample(q[1].reshape(b1, t1, hd), k_new[1].reshape(b1, t1, hd),
                              v_new[1].reshape(b1, t1, hd), cache_k, cache_v, page_table,
                              wts["b_logit_bias"][j])
        o = [o0.reshape(ms[0], hd), o1.reshape(ms[1], hd).astype(BF16)]
        x, h = unzip([matmul_res(o[g], wts["b_w_out"], j, x[g], gain=wts["norm_ffn"][layer])
                      for g in groups])
        x, h = ffn(x, h, layer)

    y = [h[g].reshape(dims[g]) for g in groups]
    k_out = [k_new[g].reshape(dims[g][0], dims[g][1], N_HEADS, HEAD_DIM) for g in groups]
    v_out = [v_new[g].reshape(dims[g][0], dims[g][1], N_HEADS, HEAD_DIM) for g in groups]
    return y, k_out, v_out, z_states


def kernel(x_prompt, x_sample, cache_k, cache_v, page_table, norm_mix, norm_ffn,
           w_ffn_up, w_ffn_down, a_w_in, a_norm_v, a_w_spatial, a_b_spatial, a_w_out,
           kv_norm, w_kv, b_w_q, b_logit_bias, b_w_out, norm_final):
    wts = dict(
        norm_mix=norm_mix, norm_ffn=norm_ffn,
        w_ffn_up=w_ffn_up, w_ffn_down=w_ffn_down.astype(BF16),
        a_w_in=a_w_in, a_norm_v=a_norm_v, a_w_spatial=a_w_spatial,
        a_b_spatial=a_b_spatial, a_w_out=a_w_out.astype(BF16), kv_norm=kv_norm,
        w_kv=w_kv, b_w_q=b_w_q, b_logit_bias=b_logit_bias,
        b_w_out=b_w_out.astype(BF16), norm_final=norm_final)

    y, k_out, v_out, z_states = _trunk([x_prompt, x_sample], (cache_k, cache_v, page_table), wts)
    db, ts, d = x_sample.shape
    sgu_v_sample = jnp.stack(z_states).reshape(len(z_states), db, ts, d)
    return (y[0], y[1], k_out[0], v_out[0], k_out[1], v_out[1], sgu_v_sample)
```

```python
import functools

import jax
import jax.numpy as jnp
from jax import lax
from jax.experimental import pallas as pl
from jax.experimental.pallas import tpu as pltpu

N_HEADS = 16
HEAD_DIM = 128
CHUNK = 128
SGU_GROUPS = 8
RMS_EPS = 1e-6
LOG2E = 1.4426950408889634
PAGE_PITCH = 24
PAGE_SLOTS = 3

VMEM_LIMIT_BYTES = 56 * 1024 * 1024

F32 = jnp.float32
BF16 = jnp.bfloat16


def _params(*semantics):
    return pltpu.CompilerParams(dimension_semantics=semantics,
                                vmem_limit_bytes=VMEM_LIMIT_BYTES)


def _rmsnorm_kernel(x_ref, g_ref, o_ref):
    x = x_ref[...]
    ms = jnp.mean(x * x, axis=-1, keepdims=True)
    o_ref[...] = (x * lax.rsqrt(ms + RMS_EPS) * g_ref[...]).astype(o_ref.dtype)


def rmsnorm(x, g, out_dtype):
    m, d = x.shape
    tm = min(m, 512)
    assert m % tm == 0
    return pl.pallas_call(
        _rmsnorm_kernel,
        grid=(m // tm,),
        in_specs=[pl.BlockSpec((tm, d), lambda i: (i, 0)),
                  pl.BlockSpec((1, d), lambda i: (0, 0))],
        out_specs=pl.BlockSpec((tm, d), lambda i: (i, 0)),
        out_shape=jax.ShapeDtypeStruct((m, d), out_dtype),
        compiler_params=_params("parallel"),
        name="rmsnorm",
    )(x, g.reshape(1, d))


def _mm_ws_kernel(*refs, epilogue, out_scale, joint, bf16_copy):
    refs = list(refs)
    a_ref = refs.pop(0)
    a2_ref = refs.pop(0) if joint else None
    w_ref = refs.pop(0)
    o_ref = refs.pop(0)
    o2_ref = refs.pop(0) if joint else None
    ob_ref = refs.pop(0) if bf16_copy else None
    (wb_ref,) = refs
    i = pl.program_id(1)

    @pl.when(i == 0)
    def _():
        wb_ref[...] = w_ref[...].astype(BF16)

    def compute(a):
        acc = jnp.dot(a, wb_ref[...], preferred_element_type=F32)
        if epilogue == "relu2":
            h = jnp.maximum(acc, 0.0)
            acc = h * h
        if out_scale is not None:
            acc = acc * out_scale
        return acc

    def main():
        acc = compute(a_ref[...])
        o_ref[...] = acc.astype(o_ref.dtype)
        if bf16_copy:
            ob_ref[...] = acc.astype(BF16)

    if not joint:
        main()
    else:
        @pl.when(i == 0)
        def _():
            o2_ref[...] = compute(a2_ref[...]).astype(o2_ref.dtype)

        pl.when(i > 0)(main)


def matmul_ws(a, w, layer, epilogue, *, a2=None, col0=0, n=None, out_scale=None, bf16_copy=False):
    m, kdim = a.shape
    n = w.shape[2] if n is None else n
    tm = min(m, 1024)
    tn = min(n, 1024)
    assert m % tm == 0 and n % tn == 0 and col0 % tn == 0 and w.shape[1] == kdim
    jb = col0 // tn
    nm = m // tm
    joint = a2 is not None
    out_dtype = F32 if epilogue == "f32" else BF16
    first = int(joint)
    main_spec = pl.BlockSpec((tm, tn), lambda j, i: (jnp.maximum(i - first, 0), j))
    in_specs = [pl.BlockSpec((tm, kdim), lambda j, i: (jnp.maximum(i - first, 0), 0))]
    out_specs = [main_spec]
    out_shape = [jax.ShapeDtypeStruct((m, n), out_dtype)]
    args = [a]
    if joint:
        m2 = a2.shape[0]
        in_specs.append(pl.BlockSpec((m2, kdim), lambda j, i: (0, 0)))
        out_specs.append(pl.BlockSpec((m2, tn), lambda j, i: (0, j)))
        out_shape.append(jax.ShapeDtypeStruct((m2, n), out_dtype))
        args.append(a2)
    if bf16_copy:
        out_specs.append(main_spec)
        out_shape.append(jax.ShapeDtypeStruct((m, n), BF16))
    in_specs.append(pl.BlockSpec((None, kdim, tn), lambda j, i: (layer, 0, jb + j)))
    args.append(w)
    outs = pl.pallas_call(
        functools.partial(_mm_ws_kernel, epilogue=epilogue, out_scale=out_scale, joint=joint,
                          bf16_copy=bf16_copy),
        grid=(n // tn, nm + first),
        in_specs=in_specs,
        out_specs=out_specs,
        out_shape=out_shape,
        scratch_shapes=[pltpu.VMEM((kdim, tn), BF16)],
        compiler_params=_params("parallel", "arbitrary"),
        name="matmul_ws_" + epilogue,
    )(*args)
    return tuple(outs) if len(outs) > 1 else outs[0]


def _mm_res_kernel(*refs, nk, emit_x, n_norm):
    refs = list(refs)
    a_ref, w_ref, r_ref = refs[:3]
    del refs[:3]
    g_refs = [refs.pop(0) for _ in range(n_norm)]
    x_ref = refs.pop(0) if emit_x else None
    hn_refs = [refs.pop(0) for _ in range(n_norm)]
    scratch = refs

    def finish(acc):
        x = r_ref[...] + acc
        if emit_x:
            x_ref[...] = x
        if n_norm:
            y = x * lax.rsqrt(jnp.mean(x * x, axis=-1, keepdims=True) + RMS_EPS)
            for g_ref, hn_ref in zip(g_refs, hn_refs):
                hn_ref[...] = (y * g_ref[...]).astype(hn_ref.dtype)

    def part():
        return jnp.dot(a_ref[...], w_ref[...], preferred_element_type=F32)

    if nk == 1:
        finish(part())
    else:
        acc_ref = scratch[0]
        k = pl.program_id(1)

        @pl.when(k == 0)
        def _():
            acc_ref[...] = part()

        @pl.when(jnp.logical_and(k > 0, k < nk - 1))
        def _():
            acc_ref[...] += part()

        @pl.when(k == nk - 1)
        def _():
            finish(acc_ref[...] + part())


def matmul_res(a, w, layer, res, *, gains=(), norm_dtype=BF16, emit_x=True):
    m, kdim = a.shape
    d = w.shape[2]
    tm = min(m, 512)
    tk = min(kdim, 2048)
    assert m % tm == 0 and kdim % tk == 0 and res.shape == (m, d)
    nk = kdim // tk
    assert emit_x or gains

    def k_block(i, k):
        return jnp.where(i % 2 == 0, k, nk - 1 - k)

    row_spec = pl.BlockSpec((tm, d), lambda i, k: (i, 0))
    in_specs = [pl.BlockSpec((tm, tk), lambda i, k: (i, k_block(i, k))),
                pl.BlockSpec((None, tk, d), lambda i, k: (layer, k_block(i, k), 0)),
                row_spec]
    args = [a, w, res]
    out_shape, out_specs = [], []
    for g in gains:
        in_specs.append(pl.BlockSpec((1, d), lambda i, k: (0, 0)))
        args.append(g.reshape(1, d))
    if emit_x:
        out_shape.append(jax.ShapeDtypeStruct((m, d), F32))
        out_specs.append(row_spec)
    for _ in gains:
        out_shape.append(jax.ShapeDtypeStruct((m, d), norm_dtype))
        out_specs.append(row_spec)
    outs = pl.pallas_call(
        functools.partial(_mm_res_kernel, nk=nk, emit_x=emit_x, n_norm=len(gains)),
        grid=(m // tm, nk),
        in_specs=in_specs,
        out_specs=out_specs,
        out_shape=out_shape,
        scratch_shapes=[pltpu.VMEM((tm, d), F32)] if nk > 1 else [],
        compiler_params=_params("parallel", "arbitrary"),
        name="matmul_res",
    )(*args)
    return list(outs)


def _gate_kernel(u_ref, v_ref, g_ref, w_ref, b_ref, p_ref, *z_refs, L, sub, nchunk):
    v = v_ref[...]
    ms = jnp.mean(v * v, axis=-1, keepdims=True)
    z = v * lax.rsqrt(ms + RMS_EPS) * g_ref[...]
    if z_refs:
        z_refs[0][...] = z
    zb = z.astype(BF16)
    gdim = v.shape[1] // SGU_GROUPS
    row = lax.broadcasted_iota(jnp.int32, (L, L), 0)
    col = lax.broadcasted_iota(jnp.int32, (L, L), 1)
    mask = jnp.logical_and(row // sub == col // sub, col % sub <= row % sub)
    for g in range(SGU_GROUPS):
        wg = jnp.where(mask, w_ref[g], 0.0).astype(BF16)
        cs = slice(g * gdim, (g + 1) * gdim)
        for c in range(nchunk):
            rs = slice(c * L, (c + 1) * L)
            s = jnp.dot(wg, zb[rs, cs], preferred_element_type=F32) + b_ref[:, cs]
            p_ref[rs, cs] = (u_ref[rs, cs] * s).astype(BF16)


def spatial_gate(uv, g_v, w_tile, b_full, *, sub, emit_z):
    m, e2 = uv.shape
    e = e2 // 2
    L = w_tile.shape[1]
    rows = min(m, 512)
    assert m % rows == 0 and rows % L == 0
    nchunk = rows // L
    out_shape = [jax.ShapeDtypeStruct((m, e), BF16)]
    out_specs = [pl.BlockSpec((rows, e), lambda i: (i, 0))]
    if emit_z:
        out_shape.append(jax.ShapeDtypeStruct((m, e), F32))
        out_specs.append(pl.BlockSpec((rows, e), lambda i: (i, 0)))
    outs = pl.pallas_call(
        functools.partial(_gate_kernel, L=L, sub=sub, nchunk=nchunk),
        grid=(m // rows,),
        in_specs=[pl.BlockSpec((rows, e), lambda i: (i, 0)),
                  pl.BlockSpec((rows, e), lambda i: (i, 1)),
                  pl.BlockSpec((1, e), lambda i: (0, 0)),
                  pl.BlockSpec(w_tile.shape, lambda i: (0, 0, 0)),
                  pl.BlockSpec((L, e), lambda i: (0, 0))],
        out_specs=out_specs,
        out_shape=out_shape,
        compiler_params=_params("parallel"),
        name="spatial_gate",
    )(uv, uv, g_v.reshape(1, e), w_tile, b_full)
    return (outs[0], outs[1]) if emit_z else (outs[0], None)


def _sb_core(z, mask, c, tri, key_axis, split):
    lr = -(jnp.maximum(z, 0.0) + jnp.log2(1.0 + jnp.exp2(-jnp.abs(z))))
    if mask is not None:
        lr = jnp.where(mask, lr, 0.0)
    lrb = lr.astype(BF16)
    if split:
        lrb = jnp.concatenate([lrb, (lr - lrb.astype(F32)).astype(BF16)], axis=key_axis)
    if key_axis == 1:
        incl = jnp.dot(lrb, tri, preferred_element_type=F32)
        total = incl[:, 0:1]
    else:
        incl = jnp.dot(tri, lrb, preferred_element_type=F32)
        total = incl[0:1, :]
    a = jnp.exp2(z + incl + c)
    if mask is not None:
        a = jnp.where(mask, a, 0.0)
    return a, c + total


def _suffix_ones(tk, key_axis, copies):
    shape = (copies * tk, tk) if key_axis == 1 else (tk, copies * tk)
    j = lax.broadcasted_iota(jnp.int32, shape, 1 - key_axis) % tk
    s = lax.broadcasted_iota(jnp.int32, shape, key_axis)
    return (j >= s).astype(BF16)


_NT = (((1,), (1,)), ((), ()))


def _attn_prompt_kernel(bias_ref, q_ref, kb_ref, vb_ref, o_ref, uu_ref, *, tq, tk, hps):
    hp = pl.program_id(1)
    qs = pl.program_id(2)
    nsub = tq // tk
    heads = [slice(s * HEAD_DIM, (s + 1) * HEAD_DIM) for s in range(hps)]

    @pl.when(qs == 0)
    def _():
        uu_ref[...] = _suffix_ones(tk, 1, 1)

    q = [q_ref[:, hs] for hs in heads]
    bias = [bias_ref[hp * hps + s] * LOG2E for s in range(hps)]
    row = lax.broadcasted_iota(jnp.int32, (tq, tk), 0)
    col = lax.broadcasted_iota(jnp.int32, (tq, tk), 1)

    def block(kj, diag, carry):
        start = pl.multiple_of(kj * tk, tk)
        mask = None if diag is None else (col + diag * tk < row)
        out = []
        for s, hs in enumerate(heads):
            c, acc = carry[s]
            z = lax.dot_general(q[s], kb_ref[pl.ds(start, tk), hs], _NT,
                                preferred_element_type=F32) + bias[s]
            a, c = _sb_core(z, mask, c, uu_ref[...], 1, split=False)
            acc = acc + jnp.dot(a.astype(BF16), vb_ref[pl.ds(start, tk), hs],
                                preferred_element_type=F32)
            out.append((c, acc))
        return tuple(out)

    def group(first, diag, carry):
        for d in reversed(range(nsub)):
            carry = block(first + d, d if diag else None, carry)
        return carry

    zero = (jnp.zeros((tq, 1), F32), jnp.zeros((tq, HEAD_DIM), F32))
    carry = group(qs * nsub, True, (zero,) * hps)
    carry = lax.fori_loop(0, qs, lambda it, ca: group((qs - 1 - it) * nsub, False, ca), carry)
    for s, hs in enumerate(heads):
        o_ref[:, hs] = carry[s][1].astype(o_ref.dtype)


def attention_prompt(q, k, v, bias, *, tq=512, tk=256, hps=8):
    b, t, _ = q.shape
    tq = min(tq, t)
    tk = min(tk, tq)
    assert t % tq == 0 and tq % tk == 0 and N_HEADS % hps == 0
    w = hps * HEAD_DIM
    return pl.pallas_call(
        functools.partial(_attn_prompt_kernel, tq=tq, tk=tk, hps=hps),
        grid=(b, N_HEADS // hps, t // tq),
        in_specs=[pl.BlockSpec(memory_space=pltpu.SMEM),
                  pl.BlockSpec((None, tq, w), lambda bi, h, qi: (bi, qi, h)),
                  pl.BlockSpec((None, t, w), lambda bi, h, qi: (bi, 0, h)),
                  pl.BlockSpec((None, t, w), lambda bi, h, qi: (bi, 0, h))],
        out_specs=pl.BlockSpec((None, tq, w), lambda bi, h, qi: (bi, qi, h)),
        out_shape=jax.ShapeDtypeStruct(q.shape, BF16),
        scratch_shapes=[pltpu.VMEM((tk, tk), BF16)],
        compiler_params=_params("parallel", "parallel", "arbitrary"),
        name="attention_prompt",
    )(bias, q, k, v)


def _attn_sample_kernel(pt_ref, qt_ref, kn_ref, vn_ref, bias_ref, ck_hbm, cv_hbm, o_ref,
                        qbd_ref, acc_ref, c_ref, ll_ref, kb_ref, kbuf, vbuf, sems,
                        *, tq, page, ppstep, n_pages):
    bi = pl.program_id(0)
    j = pl.program_id(1)
    nsteps = pl.num_programs(1)
    cols = N_HEADS * tq
    d = N_HEADS * HEAD_DIM

    def page_copies(step, slot):
        copies = []
        for i in range(ppstep):
            pidx = pt_ref[bi, n_pages - 1 - ((step - 1) * ppstep + i)]
            for t, (hbm, buf) in enumerate(((ck_hbm, kbuf), (cv_hbm, vbuf))):
                copies.append(pltpu.make_async_copy(
                    hbm.at[pidx], buf.at[slot, i, :, pl.ds(0, N_HEADS), :], sems.at[slot, t, i]))
        return copies

    @pl.when(j == 0)
    def _():
        for step in range(1, PAGE_SLOTS):
            @pl.when(step < nsteps)
            def _():
                for cp in page_copies(step, step % PAGE_SLOTS):
                    cp.start()
        rh = lax.broadcasted_iota(jnp.int32, (d, cols), 0) // HEAD_DIM
        ch = lax.broadcasted_iota(jnp.int32, (d, cols), 1) // tq
        qbd_ref[...] = jnp.where(rh == ch, qt_ref[...], 0.0).astype(BF16)
        acc_ref[...] = jnp.zeros_like(acc_ref)
        c_ref[...] = jnp.zeros_like(c_ref)
        ll_ref[...] = _suffix_ones(page, 0, 2)

    def attend(kb, v_head, n, masked):
        z_all = jnp.dot(kb, qbd_ref[...], preferred_element_type=F32)
        mask = None
        if masked:
            s_idx = lax.broadcasted_iota(jnp.int32, (page, cols), 0)
            t_idx = lax.broadcasted_iota(jnp.int32, (page, cols), 1) % tq
            mask = s_idx < t_idx
        c = c_ref[...]
        p_parts = []
        for i in range(n):
            z = z_all[i * page:(i + 1) * page] + bias_ref[...] * LOG2E
            a, c = _sb_core(z, mask, c, ll_ref[...], 0, split=True)
            p_parts.append(a.T)
        c_ref[...] = c
        p = p_parts[0] if n == 1 else jnp.concatenate(p_parts, axis=1)
        for h in range(N_HEADS):
            rs = slice(h * tq, (h + 1) * tq)
            acc_ref[rs, :] += jnp.dot(p[rs].astype(BF16), v_head(h), preferred_element_type=F32)

    @pl.when(j == 0)
    def _():
        attend(kn_ref[...].astype(BF16),
               lambda h: vn_ref[:, h * HEAD_DIM:(h + 1) * HEAD_DIM].astype(BF16), 1, True)

    @pl.when(j > 0)
    def _():
        slot = j % PAGE_SLOTS
        ahead = j + PAGE_SLOTS - 1

        @pl.when(ahead < nsteps)
        def _():
            for cp in page_copies(ahead, ahead % PAGE_SLOTS):
                cp.start()

        for cp in page_copies(j, slot):
            cp.wait()

        rows_per_page = page * PAGE_PITCH
        kflat = kbuf.reshape(PAGE_SLOTS * ppstep * rows_per_page, HEAD_DIM)
        vflat = vbuf.reshape(PAGE_SLOTS * ppstep * rows_per_page, HEAD_DIM)

        def head_rows(flat, i, h):
            start = (slot * ppstep + i) * rows_per_page + h
            return flat[pl.ds(start, page, stride=PAGE_PITCH), :].astype(BF16)

        for i in range(ppstep):
            for h in range(N_HEADS):
                kb_ref[i * page:(i + 1) * page, h * HEAD_DIM:(h + 1) * HEAD_DIM] = head_rows(kflat, i, h)

        def v_head(h):
            parts = [head_rows(vflat, i, h) for i in range(ppstep)]
            return parts[0] if ppstep == 1 else jnp.concatenate(parts, axis=0)

        attend(kb_ref[...], v_head, ppstep, False)

    @pl.when(j == pl.num_programs(1) - 1)
    def _():
        for h in range(N_HEADS):
            o_ref[:, h * HEAD_DIM:(h + 1) * HEAD_DIM] = acc_ref[h * tq:(h + 1) * tq, :]


def attention_sample(q, k_new, v_new, cache_k, cache_v, page_table, bias, *, ppstep=4):
    db, tq, d = q.shape
    page = cache_k.shape[1]
    n_pages = page_table.shape[1]
    ppstep = min(ppstep, n_pages)
    assert n_pages % ppstep == 0
    cols = N_HEADS * tq
    pad = ((0, 0), (0, page - tq), (0, 0))
    kn = jnp.pad(k_new, pad)
    vn = jnp.pad(v_new, pad)
    qt = jnp.tile(jnp.swapaxes(q, 1, 2), (1, 1, N_HEADS)).astype(F32)
    bias_row = jnp.repeat(bias, tq).reshape(1, cols)

    def seq_block(bi, j, pt):
        return (bi, 0, 0)

    page_buf = pltpu.VMEM((PAGE_SLOTS, ppstep, page, PAGE_PITCH, HEAD_DIM), F32)
    grid_spec = pltpu.PrefetchScalarGridSpec(
        num_scalar_prefetch=1,
        grid=(db, n_pages // ppstep + 1),
        in_specs=[pl.BlockSpec((None, d, cols), seq_block),
                  pl.BlockSpec((None, page, d), seq_block),
                  pl.BlockSpec((None, page, d), seq_block),
                  pl.BlockSpec((1, cols), lambda bi, j, pt: (0, 0)),
                  pl.BlockSpec(memory_space=pl.ANY),
                  pl.BlockSpec(memory_space=pl.ANY)],
        out_specs=pl.BlockSpec((None, tq, d), seq_block),
        scratch_shapes=[pltpu.VMEM((d, cols), BF16),
                        pltpu.VMEM((cols, HEAD_DIM), F32),
                        pltpu.VMEM((1, cols), F32),
                        pltpu.VMEM((page, 2 * page), BF16),
                        pltpu.VMEM((ppstep * page, d), BF16),
                        page_buf, page_buf,
                        pltpu.SemaphoreType.DMA((PAGE_SLOTS, 2, ppstep))],
    )
    return pl.pallas_call(
        functools.partial(_attn_sample_kernel, tq=tq, page=page, ppstep=ppstep, n_pages=n_pages),
        grid_spec=grid_spec,
        out_shape=jax.ShapeDtypeStruct((db, tq, d), F32),
        compiler_params=_params("arbitrary", "arbitrary"),
        name="attention_sample",
    )(page_table, qt, kn, vn, bias_row, cache_k, cache_v)


def _trunk(x3s, past, wts):
    dims = [x3.shape for x3 in x3s]
    d = dims[0][2]
    ms = [bb * tt for bb, tt, _ in dims]
    x = [x3.reshape(mm, d) for x3, mm in zip(x3s, ms)]
    n_a = wts["a_w_in"].shape[0]
    depth = wts["w_ffn_up"].shape[0]
    groups = range(2)
    hd = N_HEADS * HEAD_DIM

    def per_group(fn):
        return [list(vals) for vals in zip(*[fn(g) for g in groups])]

    def ffn(x, h, layer):
        hh = matmul_ws(h[0], wts["w_ffn_up"], layer, "relu2", a2=h[1])
        if layer + 1 == depth:
            kw = dict(gains=(wts["norm_final"],), norm_dtype=F32, emit_x=False)
        elif layer + 1 == n_a:
            kw = dict(gains=(wts["norm_mix"][layer + 1], wts["kv_norm"]))
        else:
            kw = dict(gains=(wts["norm_mix"][layer + 1],))
        return per_group(lambda g: matmul_res(hh[g], wts["w_ffn_down"], layer, x[g], **kw))

    def gate(uv, layer, g):
        t, m = dims[g][1], ms[g]
        sub = min(t, CHUNK)
        L = sub if sub == CHUNK else min(m, 64)
        reps = L // sub
        w_tile = jnp.tile(wts["a_w_spatial"][layer][:, :sub, :sub], (1, reps, reps))
        b_rows = jnp.tile(wts["a_b_spatial"][layer][:, :sub].T, (reps, 1))
        b_full = jnp.repeat(b_rows, d // SGU_GROUPS, axis=1)
        return spatial_gate(uv, wts["a_norm_v"][layer], w_tile, b_full, sub=sub, emit_z=g == 1)

    z_states = []
    h = [rmsnorm(x[g], wts["norm_mix"][0], BF16) for g in groups]
    h_kv = None
    for layer in range(n_a):
        uv = matmul_ws(h[0], wts["a_w_in"], layer, "f32", a2=h[1])
        p, z = per_group(lambda g: gate(uv[g], layer, g))
        z_states.append(z[1])
        x, h = per_group(lambda g: matmul_res(p[g], wts["a_w_out"], layer, x[g],
                                              gains=(wts["norm_ffn"][layer],)))
        outs = ffn(x, h, layer)
        x, h = outs[0], outs[1]
        if layer + 1 == n_a:
            h_kv = outs[2]

    k_new = matmul_ws(h_kv[0], wts["w_kv"][None], 0, "f32", a2=h_kv[1], col0=0, n=hd, bf16_copy=True)
    v_new = matmul_ws(h_kv[0], wts["w_kv"][None], 0, "f32", a2=h_kv[1], col0=hd, n=hd, bf16_copy=True)

    q_scale = LOG2E * HEAD_DIM ** -0.5
    cache_k, cache_v, page_table = past
    for layer in range(n_a, depth):
        j = layer - n_a
        q = matmul_ws(h[0], wts["b_w_q"], j, "bf16", a2=h[1], out_scale=q_scale)
        (b0, t0, _), (b1, t1, _) = dims
        o0 = attention_prompt(q[0].reshape(b0, t0, hd), k_new[2].reshape(b0, t0, hd),
                              v_new[2].reshape(b0, t0, hd), wts["b_logit_bias"][j])
        o1 = attention_sample(q[1].reshape(b1, t1, hd), k_new[1].reshape(b1, t1, hd),
                              v_new[1].reshape(b1, t1, hd), cache_k, cache_v, page_table,
                              wts["b_logit_bias"][j])
        o = [o0.reshape(ms[0], hd), o1.reshape(ms[1], hd).astype(BF16)]
        x, h = per_group(lambda g: matmul_res(o[g], wts["b_w_out"], j, x[g],
                                              gains=(wts["norm_ffn"][layer],)))
        outs = ffn(x, h, layer)
        if layer + 1 < depth:
            x, h = outs[0], outs[1]

    y = [outs[0][g].reshape(dims[g]) for g in groups]
    k_out = [k_new[g].reshape(dims[g][0], dims[g][1], N_HEADS, HEAD_DIM) for g in groups]
    v_out = [v_new[g].reshape(dims[g][0], dims[g][1], N_HEADS, HEAD_DIM) for g in groups]
    return y, k_out, v_out, z_states


def kernel(x_prompt, x_sample, cache_k, cache_v, page_table, norm_mix, norm_ffn,
           w_ffn_up, w_ffn_down, a_w_in, a_norm_v, a_w_spatial, a_b_spatial, a_w_out,
           kv_norm, w_kv, b_w_q, b_logit_bias, b_w_out, norm_final):
    wts = dict(
        norm_mix=norm_mix, norm_ffn=norm_ffn,
        w_ffn_up=w_ffn_up, w_ffn_down=w_ffn_down.astype(BF16),
        a_w_in=a_w_in, a_norm_v=a_norm_v, a_w_spatial=a_w_spatial,
        a_b_spatial=a_b_spatial, a_w_out=a_w_out.astype(BF16), kv_norm=kv_norm,
        w_kv=w_kv, b_w_q=b_w_q, b_logit_bias=b_logit_bias,
        b_w_out=b_w_out.astype(BF16), norm_final=norm_final)

    y, k_out, v_out, z_states = _trunk([x_prompt, x_sample], (cache_k, cache_v, page_table), wts)
    db, ts, d = x_sample.shape
    sgu_v_sample = jnp.stack(z_states).reshape(len(z_states), db, ts, d)
    return (y[0], y[1], k_out[0], v_out[0], k_out[1], v_out[1], sgu_v_sample)
```

```python
import functools

import jax
import jax.numpy as jnp
from jax import lax
from jax.experimental import pallas as pl
from jax.experimental.pallas import tpu as pltpu

N_HEADS = 16
HEAD_DIM = 128
CHUNK = 128
SGU_GROUPS = 8
RMS_EPS = 1e-6
LOG2E = 1.4426950408889634
PAGE_PITCH = 24
PAGE_SLOTS = 3

VMEM_LIMIT_BYTES = 56 * 1024 * 1024

F32 = jnp.float32
BF16 = jnp.bfloat16


def _params(*semantics):
    return pltpu.CompilerParams(dimension_semantics=semantics,
                                vmem_limit_bytes=VMEM_LIMIT_BYTES)


def _rmsnorm_kernel(x_ref, g_ref, o_ref):
    x = x_ref[...]
    ms = jnp.mean(x * x, axis=-1, keepdims=True)
    o_ref[...] = (x * lax.rsqrt(ms + RMS_EPS) * g_ref[...]).astype(o_ref.dtype)


def rmsnorm(x, g, out_dtype):
    m, d = x.shape
    tm = min(m, 512)
    assert m % tm == 0
    return pl.pallas_call(
        _rmsnorm_kernel,
        grid=(m // tm,),
        in_specs=[pl.BlockSpec((tm, d), lambda i: (i, 0)),
                  pl.BlockSpec((1, d), lambda i: (0, 0))],
        out_specs=pl.BlockSpec((tm, d), lambda i: (i, 0)),
        out_shape=jax.ShapeDtypeStruct((m, d), out_dtype),
        compiler_params=_params("parallel"),
        name="rmsnorm",
    )(x, g.reshape(1, d))


def _mm_ws_kernel(*refs, epilogue, out_scale, joint, bf16_copy):
    refs = list(refs)
    a_ref = refs.pop(0)
    a2_ref = refs.pop(0) if joint else None
    w_ref = refs.pop(0)
    o_ref = refs.pop(0)
    o2_ref = refs.pop(0) if joint else None
    ob_ref = refs.pop(0) if bf16_copy else None
    (wb_ref,) = refs
    i = pl.program_id(1)

    @pl.when(i == 0)
    def _():
        wb_ref[...] = w_ref[...].astype(BF16)

    def compute(a):
        acc = jnp.dot(a, wb_ref[...], preferred_element_type=F32)
        if epilogue == "relu2":
            h = jnp.maximum(acc, 0.0)
            acc = h * h
        if out_scale is not None:
            acc = acc * out_scale
        return acc

    def main():
        acc = compute(a_ref[...])
        o_ref[...] = acc.astype(o_ref.dtype)
        if bf16_copy:
            ob_ref[...] = acc.astype(BF16)

    if not joint:
        main()
    else:
        @pl.when(i == 0)
        def _():
            o2_ref[...] = compute(a2_ref[...]).astype(o2_ref.dtype)

        pl.when(i > 0)(main)


def matmul_ws(a, w, layer, epilogue, *, a2=None, col0=0, n=None, out_scale=None, bf16_copy=False):
    m, kdim = a.shape
    n = w.shape[2] if n is None else n
    tm = min(m, 1024)
    tn = min(n, 1024)
    assert m % tm == 0 and n % tn == 0 and col0 % tn == 0 and w.shape[1] == kdim
    jb = col0 // tn
    nm = m // tm
    joint = a2 is not None
    out_dtype = F32 if epilogue == "f32" else BF16
    first = int(joint)
    main_spec = pl.BlockSpec((tm, tn), lambda j, i: (jnp.maximum(i - first, 0), j))
    in_specs = [pl.BlockSpec((tm, kdim), lambda j, i: (jnp.maximum(i - first, 0), 0))]
    out_specs = [main_spec]
    out_shape = [jax.ShapeDtypeStruct((m, n), out_dtype)]
    args = [a]
    if joint:
        m2 = a2.shape[0]
        in_specs.append(pl.BlockSpec((m2, kdim), lambda j, i: (0, 0)))
        out_specs.append(pl.BlockSpec((m2, tn), lambda j, i: (0, j)))
        out_shape.append(jax.ShapeDtypeStruct((m2, n), out_dtype))
        args.append(a2)
    if bf16_copy:
        out_specs.append(main_spec)
        out_shape.append(jax.ShapeDtypeStruct((m, n), BF16))
    in_specs.append(pl.BlockSpec((None, kdim, tn), lambda j, i: (layer, 0, jb + j)))
    args.append(w)
    outs = pl.pallas_call(
        functools.partial(_mm_ws_kernel, epilogue=epilogue, out_scale=out_scale, joint=joint,
                          bf16_copy=bf16_copy),
        grid=(n // tn, nm + first),
        in_specs=in_specs,
        out_specs=out_specs,
        out_shape=out_shape,
        scratch_shapes=[pltpu.VMEM((kdim, tn), BF16)],
        compiler_params=_params("parallel", "arbitrary"),
        name="matmul_ws_" + epilogue,
    )(*args)
    return tuple(outs) if len(outs) > 1 else outs[0]


def _mm_res_kernel(*refs, nk, emit_x, n_norm):
    refs = list(refs)
    a_ref, w_ref, r_ref = refs[:3]
    del refs[:3]
    g_refs = [refs.pop(0) for _ in range(n_norm)]
    x_ref = refs.pop(0) if emit_x else None
    hn_refs = [refs.pop(0) for _ in range(n_norm)]
    scratch = refs

    def finish(acc):
        x = r_ref[...] + acc
        if emit_x:
            x_ref[...] = x
        if n_norm:
            y = x * lax.rsqrt(jnp.mean(x * x, axis=-1, keepdims=True) + RMS_EPS)
            for g_ref, hn_ref in zip(g_refs, hn_refs):
                hn_ref[...] = (y * g_ref[...]).astype(hn_ref.dtype)

    def part():
        return jnp.dot(a_ref[...], w_ref[...], preferred_element_type=F32)

    if nk == 1:
        finish(part())
    else:
        acc_ref = scratch[0]
        k = pl.program_id(1)

        @pl.when(k == 0)
        def _():
            acc_ref[...] = part()

        @pl.when(jnp.logical_and(k > 0, k < nk - 1))
        def _():
            acc_ref[...] += part()

        @pl.when(k == nk - 1)
        def _():
            finish(acc_ref[...] + part())


def matmul_res(a, w, layer, res, *, gains=(), norm_dtype=BF16, emit_x=True):
    m, kdim = a.shape
    d = w.shape[2]
    tm = min(m, 512)
    tk = min(kdim, 2048)
    assert m % tm == 0 and kdim % tk == 0 and res.shape == (m, d)
    nk = kdim // tk
    assert emit_x or gains

    def k_block(i, k):
        return jnp.where(i % 2 == 0, k, nk - 1 - k)

    row_spec = pl.BlockSpec((tm, d), lambda i, k: (i, 0))
    in_specs = [pl.BlockSpec((tm, tk), lambda i, k: (i, k_block(i, k))),
                pl.BlockSpec((None, tk, d), lambda i, k: (layer, k_block(i, k), 0)),
                row_spec]
    args = [a, w, res]
    out_shape, out_specs = [], []
    for g in gains:
        in_specs.append(pl.BlockSpec((1, d), lambda i, k: (0, 0)))
        args.append(g.reshape(1, d))
    if emit_x:
        out_shape.append(jax.ShapeDtypeStruct((m, d), F32))
        out_specs.append(row_spec)
    for _ in gains:
        out_shape.append(jax.ShapeDtypeStruct((m, d), norm_dtype))
        out_specs.append(row_spec)
    outs = pl.pallas_call(
        functools.partial(_mm_res_kernel, nk=nk, emit_x=emit_x, n_norm=len(gains)),
        grid=(m // tm, nk),
        in_specs=in_specs,
        out_specs=out_specs,
        out_shape=out_shape,
        scratch_shapes=[pltpu.VMEM((tm, d), F32)] if nk > 1 else [],
        compiler_params=_params("parallel", "arbitrary"),
        name="matmul_res",
    )(*args)
    return list(outs)


def _gate_kernel(u_ref, v_ref, g_ref, w_ref, b_ref, p_ref, *z_refs, L, sub, nchunk):
    v = v_ref[...]
    ms = jnp.mean(v * v, axis=-1, keepdims=True)
    z = v * lax.rsqrt(ms + RMS_EPS) * g_ref[...]
    if z_refs:
        z_refs[0][...] = z
    zb = z.astype(BF16)
    gdim = v.shape[1] // SGU_GROUPS
    row = lax.broadcasted_iota(jnp.int32, (L, L), 0)
    col = lax.broadcasted_iota(jnp.int32, (L, L), 1)
    mask = jnp.logical_and(row // sub == col // sub, col % sub <= row % sub)
    for g in range(SGU_GROUPS):
        wg = jnp.where(mask, w_ref[g], 0.0).astype(BF16)
        cs = slice(g * gdim, (g + 1) * gdim)
        for c in range(nchunk):
            rs = slice(c * L, (c + 1) * L)
            s = jnp.dot(wg, zb[rs, cs], preferred_element_type=F32) + b_ref[:, cs]
            p_ref[rs, cs] = (u_ref[rs, cs] * s).astype(BF16)


def spatial_gate(uv, g_v, w_tile, b_full, *, sub, emit_z):
    m, e2 = uv.shape
    e = e2 // 2
    L = w_tile.shape[1]
    rows = min(m, 512)
    assert m % rows == 0 and rows % L == 0
    nchunk = rows // L
    out_shape = [jax.ShapeDtypeStruct((m, e), BF16)]
    out_specs = [pl.BlockSpec((rows, e), lambda i: (i, 0))]
    if emit_z:
        out_shape.append(jax.ShapeDtypeStruct((m, e), F32))
        out_specs.append(pl.BlockSpec((rows, e), lambda i: (i, 0)))
    outs = pl.pallas_call(
        functools.partial(_gate_kernel, L=L, sub=sub, nchunk=nchunk),
        grid=(m // rows,),
        in_specs=[pl.BlockSpec((rows, e), lambda i: (i, 0)),
                  pl.BlockSpec((rows, e), lambda i: (i, 1)),
                  pl.BlockSpec((1, e), lambda i: (0, 0)),
                  pl.BlockSpec(w_tile.shape, lambda i: (0, 0, 0)),
                  pl.BlockSpec((L, e), lambda i: (0, 0))],
        out_specs=out_specs,
        out_shape=out_shape,
        compiler_params=_params("parallel"),
        name="spatial_gate",
    )(uv, uv, g_v.reshape(1, e), w_tile, b_full)
    return (outs[0], outs[1]) if emit_z else (outs[0], None)


def _sb_core(z, mask, c, tri, key_axis, split):
    lr = -(jnp.maximum(z, 0.0) + jnp.log2(1.0 + jnp.exp2(-jnp.abs(z))))
    if mask is not None:
        lr = jnp.where(mask, lr, 0.0)
    lrb = lr.astype(BF16)
    if split:
        lrb = jnp.concatenate([lrb, (lr - lrb.astype(F32)).astype(BF16)], axis=key_axis)
    if key_axis == 1:
        incl = jnp.dot(lrb, tri, preferred_element_type=F32)
        total = incl[:, 0:1]
    else:
        incl = jnp.dot(tri, lrb, preferred_element_type=F32)
        total = incl[0:1, :]
    a = jnp.exp2(z + incl + c)
    if mask is not None:
        a = jnp.where(mask, a, 0.0)
    return a, c + total


def _suffix_ones(tk, key_axis, copies):
    shape = (copies * tk, tk) if key_axis == 1 else (tk, copies * tk)
    j = lax.broadcasted_iota(jnp.int32, shape, 1 - key_axis) % tk
    s = lax.broadcasted_iota(jnp.int32, shape, key_axis)
    return (j >= s).astype(BF16)


_NT = (((1,), (1,)), ((), ()))


def _attn_prompt_kernel(bias_ref, q_ref, kb_ref, vb_ref, o_ref, uu_ref, *, tq, tk, hps):
    hp = pl.program_id(1)
    qs = pl.program_id(2)
    nsub = tq // tk
    heads = [slice(s * HEAD_DIM, (s + 1) * HEAD_DIM) for s in range(hps)]

    @pl.when(qs == 0)
    def _():
        uu_ref[...] = _suffix_ones(tk, 1, 1)

    q = [q_ref[:, hs] for hs in heads]
    bias = [bias_ref[hp * hps + s] * LOG2E for s in range(hps)]
    row = lax.broadcasted_iota(jnp.int32, (tq, tk), 0)
    col = lax.broadcasted_iota(jnp.int32, (tq, tk), 1)

    def block(kj, diag, carry):
        start = pl.multiple_of(kj * tk, tk)
        mask = None if diag is None else (col + diag * tk < row)
        out = []
        for s, hs in enumerate(heads):
            c, acc = carry[s]
            z = lax.dot_general(q[s], kb_ref[pl.ds(start, tk), hs], _NT,
                                preferred_element_type=F32) + bias[s]
            a, c = _sb_core(z, mask, c, uu_ref[...], 1, split=False)
            acc = acc + jnp.dot(a.astype(BF16), vb_ref[pl.ds(start, tk), hs],
                                preferred_element_type=F32)
            out.append((c, acc))
        return tuple(out)

    def group(first, diag, carry):
        for d in reversed(range(nsub)):
            carry = block(first + d, d if diag else None, carry)
        return carry

    zero = (jnp.zeros((tq, 1), F32), jnp.zeros((tq, HEAD_DIM), F32))
    carry = group(qs * nsub, True, (zero,) * hps)
    carry = lax.fori_loop(0, qs, lambda it, ca: group((qs - 1 - it) * nsub, False, ca), carry)
    for s, hs in enumerate(heads):
        o_ref[:, hs] = carry[s][1].astype(o_ref.dtype)


def attention_prompt(q, k, v, bias, *, tq=512, tk=256, hps=8):
    b, t, _ = q.shape
    tq = min(tq, t)
    tk = min(tk, tq)
    assert t % tq == 0 and tq % tk == 0 and N_HEADS % hps == 0
    w = hps * HEAD_DIM
    return pl.pallas_call(
        functools.partial(_attn_prompt_kernel, tq=tq, tk=tk, hps=hps),
        grid=(b, N_HEADS // hps, t // tq),
        in_specs=[pl.BlockSpec(memory_space=pltpu.SMEM),
                  pl.BlockSpec((None, tq, w), lambda bi, h, qi: (bi, qi, h)),
                  pl.BlockSpec((None, t, w), lambda bi, h, qi: (bi, 0, h)),
                  pl.BlockSpec((None, t, w), lambda bi, h, qi: (bi, 0, h))],
        out_specs=pl.BlockSpec((None, tq, w), lambda bi, h, qi: (bi, qi, h)),
        out_shape=jax.ShapeDtypeStruct(q.shape, BF16),
        scratch_shapes=[pltpu.VMEM((tk, tk), BF16)],
        compiler_params=_params("parallel", "parallel", "arbitrary"),
        name="attention_prompt",
    )(bias, q, k, v)


def _attn_sample_kernel(pt_ref, qt_ref, kn_ref, vn_ref, bias_ref, ck_hbm, cv_hbm, o_ref,
                        qbd_ref, acc_ref, c_ref, ll_ref, kb_ref, kbuf, vbuf, sems,
                        *, tq, page, ppstep, n_pages):
    bi = pl.program_id(0)
    j = pl.program_id(1)
    nsteps = pl.num_programs(1)
    cols = N_HEADS * tq
    d = N_HEADS * HEAD_DIM

    def page_copies(step, slot):
        copies = []
        for i in range(ppstep):
            pidx = pt_ref[bi, n_pages - 1 - ((step - 1) * ppstep + i)]
            for t, (hbm, buf) in enumerate(((ck_hbm, kbuf), (cv_hbm, vbuf))):
                copies.append(pltpu.make_async_copy(
                    hbm.at[pidx], buf.at[slot, i, :, pl.ds(0, N_HEADS), :], sems.at[slot, t, i]))
        return copies

    @pl.when(j == 0)
    def _():
        for step in range(1, PAGE_SLOTS):
            @pl.when(step < nsteps)
            def _():
                for cp in page_copies(step, step % PAGE_SLOTS):
                    cp.start()
        rh = lax.broadcasted_iota(jnp.int32, (d, cols), 0) // HEAD_DIM
        ch = lax.broadcasted_iota(jnp.int32, (d, cols), 1) // tq
        qbd_ref[...] = jnp.where(rh == ch, qt_ref[...], jnp.zeros((), BF16))
        acc_ref[...] = jnp.zeros_like(acc_ref)
        c_ref[...] = jnp.zeros_like(c_ref)
        ll_ref[...] = _suffix_ones(page, 0, 2)

    def attend(kb, v_head, n, masked):
        z_all = jnp.dot(kb, qbd_ref[...], preferred_element_type=F32)
        mask = None
        if masked:
            s_idx = lax.broadcasted_iota(jnp.int32, (page, cols), 0)
            t_idx = lax.broadcasted_iota(jnp.int32, (page, cols), 1) % tq
            mask = s_idx < t_idx
        c = c_ref[...]
        p_parts = []
        for i in range(n):
            z = z_all[i * page:(i + 1) * page] + bias_ref[...] * LOG2E
            a, c = _sb_core(z, mask, c, ll_ref[...], 0, split=True)
            p_parts.append(a.T)
        c_ref[...] = c
        p = p_parts[0] if n == 1 else jnp.concatenate(p_parts, axis=1)
        for h in range(N_HEADS):
            rs = slice(h * tq, (h + 1) * tq)
            acc_ref[rs, :] += jnp.dot(p[rs].astype(BF16), v_head(h), preferred_element_type=F32)

    @pl.when(j == 0)
    def _():
        def padded(rows):
            zeros = jnp.zeros((page - tq, rows.shape[1]), BF16)
            return jnp.concatenate([rows.astype(BF16), zeros], axis=0)

        attend(padded(kn_ref[...]),
               lambda h: padded(vn_ref[:, h * HEAD_DIM:(h + 1) * HEAD_DIM]), 1, True)

    @pl.when(j > 0)
    def _():
        slot = j % PAGE_SLOTS
        ahead = j + PAGE_SLOTS - 1

        @pl.when(ahead < nsteps)
        def _():
            for cp in page_copies(ahead, ahead % PAGE_SLOTS):
                cp.start()

        for cp in page_copies(j, slot):
            cp.wait()

        rows_per_page = page * PAGE_PITCH
        kflat = kbuf.reshape(PAGE_SLOTS * ppstep * rows_per_page, HEAD_DIM)
        vflat = vbuf.reshape(PAGE_SLOTS * ppstep * rows_per_page, HEAD_DIM)

        def head_rows(flat, i, h):
            start = (slot * ppstep + i) * rows_per_page + h
            return flat[pl.ds(start, page, stride=PAGE_PITCH), :].astype(BF16)

        for i in range(ppstep):
            for h in range(N_HEADS):
                kb_ref[i * page:(i + 1) * page, h * HEAD_DIM:(h + 1) * HEAD_DIM] = head_rows(kflat, i, h)

        def v_head(h):
            parts = [head_rows(vflat, i, h) for i in range(ppstep)]
            return parts[0] if ppstep == 1 else jnp.concatenate(parts, axis=0)

        attend(kb_ref[...], v_head, ppstep, False)

    @pl.when(j == pl.num_programs(1) - 1)
    def _():
        for h in range(N_HEADS):
            o_ref[:, h * HEAD_DIM:(h + 1) * HEAD_DIM] = acc_ref[h * tq:(h + 1) * tq, :]


def attention_sample(q, k_new, v_new, cache_k, cache_v, page_table, bias, *, ppstep=4):
    db, tq, d = q.shape
    page = cache_k.shape[1]
    n_pages = page_table.shape[1]
    ppstep = min(ppstep, n_pages)
    assert n_pages % ppstep == 0
    cols = N_HEADS * tq
    qt = jnp.tile(jnp.swapaxes(q, 1, 2), (1, 1, N_HEADS))
    bias_row = jnp.repeat(bias, tq).reshape(1, cols)

    def seq_block(bi, j, pt):
        return (bi, 0, 0)

    page_buf = pltpu.VMEM((PAGE_SLOTS, ppstep, page, PAGE_PITCH, HEAD_DIM), F32)
    grid_spec = pltpu.PrefetchScalarGridSpec(
        num_scalar_prefetch=1,
        grid=(db, n_pages // ppstep + 1),
        in_specs=[pl.BlockSpec((None, d, cols), seq_block),
                  pl.BlockSpec((None, tq, d), seq_block),
                  pl.BlockSpec((None, tq, d), seq_block),
                  pl.BlockSpec((1, cols), lambda bi, j, pt: (0, 0)),
                  pl.BlockSpec(memory_space=pl.ANY),
                  pl.BlockSpec(memory_space=pl.ANY)],
        out_specs=pl.BlockSpec((None, tq, d), seq_block),
        scratch_shapes=[pltpu.VMEM((d, cols), BF16),
                        pltpu.VMEM((cols, HEAD_DIM), F32),
                        pltpu.VMEM((1, cols), F32),
                        pltpu.VMEM((page, 2 * page), BF16),
                        pltpu.VMEM((ppstep * page, d), BF16),
                        page_buf, page_buf,
                        pltpu.SemaphoreType.DMA((PAGE_SLOTS, 2, ppstep))],
    )
    return pl.pallas_call(
        functools.partial(_attn_sample_kernel, tq=tq, page=page, ppstep=ppstep, n_pages=n_pages),
        grid_spec=grid_spec,
        out_shape=jax.ShapeDtypeStruct((db, tq, d), F32),
        compiler_params=_params("arbitrary", "arbitrary"),
        name="attention_sample",
    )(page_table, qt, k_new, v_new, bias_row, cache_k, cache_v)


def _trunk(x3s, past, wts):
    dims = [x3.shape for x3 in x3s]
    d = dims[0][2]
    ms = [bb * tt for bb, tt, _ in dims]
    x = [x3.reshape(mm, d) for x3, mm in zip(x3s, ms)]
    n_a = wts["a_w_in"].shape[0]
    depth = wts["w_ffn_up"].shape[0]
    groups = range(2)
    hd = N_HEADS * HEAD_DIM

    def per_group(fn):
        return [list(vals) for vals in zip(*[fn(g) for g in groups])]

    def ffn(x, h, layer):
        hh = matmul_ws(h[0], wts["w_ffn_up"], layer, "relu2", a2=h[1])
        if layer + 1 == depth:
            kw = dict(gains=(wts["norm_final"],), norm_dtype=F32, emit_x=False)
        elif layer + 1 == n_a:
            kw = dict(gains=(wts["norm_mix"][layer + 1], wts["kv_norm"]))
        else:
            kw = dict(gains=(wts["norm_mix"][layer + 1],))
        return per_group(lambda g: matmul_res(hh[g], wts["w_ffn_down"], layer, x[g], **kw))

    def gate(uv, layer, g):
        t, m = dims[g][1], ms[g]
        sub = min(t, CHUNK)
        L = sub if sub == CHUNK else min(m, 64)
        reps = L // sub
        w_tile = jnp.tile(wts["a_w_spatial"][layer][:, :sub, :sub], (1, reps, reps))
        b_rows = jnp.tile(wts["a_b_spatial"][layer][:, :sub].T, (reps, 1))
        b_full = jnp.repeat(b_rows, d // SGU_GROUPS, axis=1)
        return spatial_gate(uv, wts["a_norm_v"][layer], w_tile, b_full, sub=sub, emit_z=g == 1)

    z_states = []
    h = [rmsnorm(x[g], wts["norm_mix"][0], BF16) for g in groups]
    h_kv = None
    for layer in range(n_a):
        uv = matmul_ws(h[0], wts["a_w_in"], layer, "f32", a2=h[1])
        p, z = per_group(lambda g: gate(uv[g], layer, g))
        z_states.append(z[1])
        x, h = per_group(lambda g: matmul_res(p[g], wts["a_w_out"], layer, x[g],
                                              gains=(wts["norm_ffn"][layer],)))
        outs = ffn(x, h, layer)
        x, h = outs[0], outs[1]
        if layer + 1 == n_a:
            h_kv = outs[2]

    k_new = matmul_ws(h_kv[0], wts["w_kv"][None], 0, "f32", a2=h_kv[1], col0=0, n=hd, bf16_copy=True)
    v_new = matmul_ws(h_kv[0], wts["w_kv"][None], 0, "f32", a2=h_kv[1], col0=hd, n=hd, bf16_copy=True)

    q_scale = LOG2E * HEAD_DIM ** -0.5
    cache_k, cache_v, page_table = past
    for layer in range(n_a, depth):
        j = layer - n_a
        q = matmul_ws(h[0], wts["b_w_q"], j, "bf16", a2=h[1], out_scale=q_scale)
        (b0, t0, _), (b1, t1, _) = dims
        o0 = attention_prompt(q[0].reshape(b0, t0, hd), k_new[2].reshape(b0, t0, hd),
                              v_new[2].reshape(b0, t0, hd), wts["b_logit_bias"][j])
        o1 = attention_sample(q[1].reshape(b1, t1, hd), k_new[1].reshape(b1, t1, hd),
                              v_new[1].reshape(b1, t1, hd), cache_k, cache_v, page_table,
                              wts["b_logit_bias"][j])
        o = [o0.reshape(ms[0], hd), o1.reshape(ms[1], hd).astype(BF16)]
        x, h = per_group(lambda g: matmul_res(o[g], wts["b_w_out"], j, x[g],
                                              gains=(wts["norm_ffn"][layer],)))
        outs = ffn(x, h, layer)
        if layer + 1 < depth:
            x, h = outs[0], outs[1]

    y = [outs[0][g].reshape(dims[g]) for g in groups]
    k_out = [k_new[g].reshape(dims[g][0], dims[g][1], N_HEADS, HEAD_DIM) for g in groups]
    v_out = [v_new[g].reshape(dims[g][0], dims[g][1], N_HEADS, HEAD_DIM) for g in groups]
    return y, k_out, v_out, z_states


def kernel(x_prompt, x_sample, cache_k, cache_v, page_table, norm_mix, norm_ffn,
           w_ffn_up, w_ffn_down, a_w_in, a_norm_v, a_w_spatial, a_b_spatial, a_w_out,
           kv_norm, w_kv, b_w_q, b_logit_bias, b_w_out, norm_final):
    wts = dict(
        norm_mix=norm_mix, norm_ffn=norm_ffn,
        w_ffn_up=w_ffn_up, w_ffn_down=w_ffn_down.astype(BF16),
        a_w_in=a_w_in, a_norm_v=a_norm_v, a_w_spatial=a_w_spatial,
        a_b_spatial=a_b_spatial, a_w_out=a_w_out.astype(BF16), kv_norm=kv_norm,
        w_kv=w_kv, b_w_q=b_w_q, b_logit_bias=b_logit_bias,
        b_w_out=b_w_out.astype(BF16), norm_final=norm_final)

    y, k_out, v_out, z_states = _trunk([x_prompt, x_sample], (cache_k, cache_v, page_table), wts)
    db, ts, d = x_sample.shape
    sgu_v_sample = jnp.stack(z_states).reshape(len(z_states), db, ts, d)
    return (y[0], y[1], k_out[0], v_out[0], k_out[1], v_out[1], sgu_v_sample)
```

```python
import functools

import jax
import jax.numpy as jnp
from jax import lax
from jax.experimental import pallas as pl
from jax.experimental.pallas import tpu as pltpu

N_HEADS = 16
HEAD_DIM = 128
CHUNK = 128
SGU_GROUPS = 8
RMS_EPS = 1e-6
LOG2E = 1.4426950408889634
PAGE_PITCH = 24
PAGE_SLOTS = 4

VMEM_LIMIT_BYTES = 56 * 1024 * 1024

F32 = jnp.float32
BF16 = jnp.bfloat16


def _params(*semantics):
    return pltpu.CompilerParams(dimension_semantics=semantics,
                                vmem_limit_bytes=VMEM_LIMIT_BYTES)


def _rmsnorm_kernel(x_ref, g_ref, o_ref):
    x = x_ref[...]
    ms = jnp.mean(x * x, axis=-1, keepdims=True)
    o_ref[...] = (x * lax.rsqrt(ms + RMS_EPS) * g_ref[...]).astype(o_ref.dtype)


def rmsnorm(x, g, out_dtype):
    m, d = x.shape
    tm = min(m, 512)
    assert m % tm == 0
    return pl.pallas_call(
        _rmsnorm_kernel,
        grid=(m // tm,),
        in_specs=[pl.BlockSpec((tm, d), lambda i: (i, 0)),
                  pl.BlockSpec((1, d), lambda i: (0, 0))],
        out_specs=pl.BlockSpec((tm, d), lambda i: (i, 0)),
        out_shape=jax.ShapeDtypeStruct((m, d), out_dtype),
        compiler_params=_params("parallel"),
        name="rmsnorm",
    )(x, g.reshape(1, d))


def _mm_ws_kernel(*refs, epilogue, out_scale, joint, bf16_copy):
    refs = list(refs)
    a_ref = refs.pop(0)
    a2_ref = refs.pop(0) if joint else None
    w_ref = refs.pop(0)
    o_ref = refs.pop(0)
    o2_ref = refs.pop(0) if joint else None
    ob_ref = refs.pop(0) if bf16_copy else None
    (wb_ref,) = refs
    i = pl.program_id(1)

    @pl.when(i == 0)
    def _():
        wb_ref[...] = w_ref[...].astype(BF16)

    def compute(a):
        acc = jnp.dot(a, wb_ref[...], preferred_element_type=F32)
        if epilogue == "relu2":
            h = jnp.maximum(acc, 0.0)
            acc = h * h
        if out_scale is not None:
            acc = acc * out_scale
        return acc

    def main():
        acc = compute(a_ref[...])
        o_ref[...] = acc.astype(o_ref.dtype)
        if bf16_copy:
            ob_ref[...] = acc.astype(BF16)

    if not joint:
        main()
    else:
        @pl.when(i == 0)
        def _():
            o2_ref[...] = compute(a2_ref[...]).astype(o2_ref.dtype)

        pl.when(i > 0)(main)


def matmul_ws(a, w, layer, epilogue, *, a2=None, col0=0, n=None, out_scale=None, bf16_copy=False):
    m, kdim = a.shape
    n = w.shape[2] if n is None else n
    tm = min(m, 1024)
    tn = min(n, 1024)
    assert m % tm == 0 and n % tn == 0 and col0 % tn == 0 and w.shape[1] == kdim
    jb = col0 // tn
    nm = m // tm
    joint = a2 is not None
    out_dtype = F32 if epilogue == "f32" else BF16
    first = int(joint)
    main_spec = pl.BlockSpec((tm, tn), lambda j, i: (jnp.maximum(i - first, 0), j))
    in_specs = [pl.BlockSpec((tm, kdim), lambda j, i: (jnp.maximum(i - first, 0), 0))]
    out_specs = [main_spec]
    out_shape = [jax.ShapeDtypeStruct((m, n), out_dtype)]
    args = [a]
    if joint:
        m2 = a2.shape[0]
        in_specs.append(pl.BlockSpec((m2, kdim), lambda j, i: (0, 0)))
        out_specs.append(pl.BlockSpec((m2, tn), lambda j, i: (0, j)))
        out_shape.append(jax.ShapeDtypeStruct((m2, n), out_dtype))
        args.append(a2)
    if bf16_copy:
        out_specs.append(main_spec)
        out_shape.append(jax.ShapeDtypeStruct((m, n), BF16))
    in_specs.append(pl.BlockSpec((None, kdim, tn), lambda j, i: (layer, 0, jb + j)))
    args.append(w)
    outs = pl.pallas_call(
        functools.partial(_mm_ws_kernel, epilogue=epilogue, out_scale=out_scale, joint=joint,
                          bf16_copy=bf16_copy),
        grid=(n // tn, nm + first),
        in_specs=in_specs,
        out_specs=out_specs,
        out_shape=out_shape,
        scratch_shapes=[pltpu.VMEM((kdim, tn), BF16)],
        compiler_params=_params("parallel", "arbitrary"),
        name="matmul_ws_" + epilogue,
    )(*args)
    return tuple(outs) if len(outs) > 1 else outs[0]


def _mm_res_kernel(*refs, nk, emit_x, n_norm):
    refs = list(refs)
    a_ref, w_ref, r_ref = refs[:3]
    del refs[:3]
    g_refs = [refs.pop(0) for _ in range(n_norm)]
    x_ref = refs.pop(0) if emit_x else None
    hn_refs = [refs.pop(0) for _ in range(n_norm)]
    scratch = refs

    def finish(acc):
        x = r_ref[...] + acc
        if emit_x:
            x_ref[...] = x
        if n_norm:
            y = x * lax.rsqrt(jnp.mean(x * x, axis=-1, keepdims=True) + RMS_EPS)
            for g_ref, hn_ref in zip(g_refs, hn_refs):
                hn_ref[...] = (y * g_ref[...]).astype(hn_ref.dtype)

    def part():
        return jnp.dot(a_ref[...], w_ref[...], preferred_element_type=F32)

    if nk == 1:
        finish(part())
    else:
        acc_ref = scratch[0]
        k = pl.program_id(1)

        @pl.when(k == 0)
        def _():
            acc_ref[...] = part()

        @pl.when(jnp.logical_and(k > 0, k < nk - 1))
        def _():
            acc_ref[...] += part()

        @pl.when(k == nk - 1)
        def _():
            finish(acc_ref[...] + part())


def matmul_res(a, w, layer, res, *, gains=(), norm_dtype=BF16, emit_x=True):
    m, kdim = a.shape
    d = w.shape[2]
    tm = min(m, 512)
    tk = min(kdim, 2048)
    assert m % tm == 0 and kdim % tk == 0 and res.shape == (m, d)
    nk = kdim // tk
    assert emit_x or gains

    def k_block(i, k):
        return jnp.where(i % 2 == 0, k, nk - 1 - k)

    row_spec = pl.BlockSpec((tm, d), lambda i, k: (i, 0))
    in_specs = [pl.BlockSpec((tm, tk), lambda i, k: (i, k_block(i, k))),
                pl.BlockSpec((None, tk, d), lambda i, k: (layer, k_block(i, k), 0)),
                row_spec]
    args = [a, w, res]
    out_shape, out_specs = [], []
    for g in gains:
        in_specs.append(pl.BlockSpec((1, d), lambda i, k: (0, 0)))
        args.append(g.reshape(1, d))
    if emit_x:
        out_shape.append(jax.ShapeDtypeStruct((m, d), F32))
        out_specs.append(row_spec)
    for _ in gains:
        out_shape.append(jax.ShapeDtypeStruct((m, d), norm_dtype))
        out_specs.append(row_spec)
    outs = pl.pallas_call(
        functools.partial(_mm_res_kernel, nk=nk, emit_x=emit_x, n_norm=len(gains)),
        grid=(m // tm, nk),
        in_specs=in_specs,
        out_specs=out_specs,
        out_shape=out_shape,
        scratch_shapes=[pltpu.VMEM((tm, d), F32)] if nk > 1 else [],
        compiler_params=_params("parallel", "arbitrary"),
        name="matmul_res",
    )(*args)
    return list(outs)


def _gate_kernel(u_ref, v_ref, g_ref, w_ref, b_ref, p_ref, *z_refs, L, sub, nchunk):
    v = v_ref[...]
    ms = jnp.mean(v * v, axis=-1, keepdims=True)
    z = v * lax.rsqrt(ms + RMS_EPS) * g_ref[...]
    if z_refs:
        z_refs[0][...] = z
    zb = z.astype(BF16)
    gdim = v.shape[1] // SGU_GROUPS
    row = lax.broadcasted_iota(jnp.int32, (L, L), 0)
    col = lax.broadcasted_iota(jnp.int32, (L, L), 1)
    mask = jnp.logical_and(row // sub == col // sub, col % sub <= row % sub)
    for g in range(SGU_GROUPS):
        wg = jnp.where(mask, w_ref[g], 0.0).astype(BF16)
        cs = slice(g * gdim, (g + 1) * gdim)
        for c in range(nchunk):
            rs = slice(c * L, (c + 1) * L)
            s = jnp.dot(wg, zb[rs, cs], preferred_element_type=F32) + b_ref[:, cs]
            p_ref[rs, cs] = (u_ref[rs, cs] * s).astype(BF16)


def spatial_gate(uv, g_v, w_tile, b_full, *, sub, emit_z):
    m, e2 = uv.shape
    e = e2 // 2
    L = w_tile.shape[1]
    rows = min(m, 512)
    assert m % rows == 0 and rows % L == 0
    nchunk = rows // L
    out_shape = [jax.ShapeDtypeStruct((m, e), BF16)]
    out_specs = [pl.BlockSpec((rows, e), lambda i: (i, 0))]
    if emit_z:
        out_shape.append(jax.ShapeDtypeStruct((m, e), F32))
        out_specs.append(pl.BlockSpec((rows, e), lambda i: (i, 0)))
    outs = pl.pallas_call(
        functools.partial(_gate_kernel, L=L, sub=sub, nchunk=nchunk),
        grid=(m // rows,),
        in_specs=[pl.BlockSpec((rows, e), lambda i: (i, 0)),
                  pl.BlockSpec((rows, e), lambda i: (i, 1)),
                  pl.BlockSpec((1, e), lambda i: (0, 0)),
                  pl.BlockSpec(w_tile.shape, lambda i: (0, 0, 0)),
                  pl.BlockSpec((L, e), lambda i: (0, 0))],
        out_specs=out_specs,
        out_shape=out_shape,
        compiler_params=_params("parallel"),
        name="spatial_gate",
    )(uv, uv, g_v.reshape(1, e), w_tile, b_full)
    return (outs[0], outs[1]) if emit_z else (outs[0], None)


def _sb_core(z, mask, c, tri, key_axis, split):
    lr = -(jnp.maximum(z, 0.0) + jnp.log2(1.0 + jnp.exp2(-jnp.abs(z))))
    if mask is not None:
        lr = jnp.where(mask, lr, 0.0)
    lrb = lr.astype(BF16)
    if split:
        lrb = jnp.concatenate([lrb, (lr - lrb.astype(F32)).astype(BF16)], axis=key_axis)
    if key_axis == 1:
        incl = jnp.dot(lrb, tri, preferred_element_type=F32)
        total = incl[:, 0:1]
    else:
        incl = jnp.dot(tri, lrb, preferred_element_type=F32)
        total = incl[0:1, :]
    a = jnp.exp2(z + incl + c)
    if mask is not None:
        a = jnp.where(mask, a, 0.0)
    return a, c + total


def _suffix_ones(tk, key_axis, copies):
    shape = (copies * tk, tk) if key_axis == 1 else (tk, copies * tk)
    j = lax.broadcasted_iota(jnp.int32, shape, 1 - key_axis) % tk
    s = lax.broadcasted_iota(jnp.int32, shape, key_axis)
    return (j >= s).astype(BF16)


_NT = (((1,), (1,)), ((), ()))


def _attn_prompt_kernel(bias_ref, q_ref, kb_ref, vb_ref, o_ref, uu_ref, *, tq, tk, hps):
    hp = pl.program_id(1)
    qs = pl.program_id(2)
    nsub = tq // tk
    heads = [slice(s * HEAD_DIM, (s + 1) * HEAD_DIM) for s in range(hps)]

    @pl.when(qs == 0)
    def _():
        uu_ref[...] = _suffix_ones(tk, 1, 1)

    q = [q_ref[:, hs] for hs in heads]
    bias = [bias_ref[hp * hps + s] * LOG2E for s in range(hps)]
    row = lax.broadcasted_iota(jnp.int32, (tq, tk), 0)
    col = lax.broadcasted_iota(jnp.int32, (tq, tk), 1)

    def block(kj, diag, carry):
        start = pl.multiple_of(kj * tk, tk)
        mask = None if diag is None else (col + diag * tk < row)
        out = []
        for s, hs in enumerate(heads):
            c, acc = carry[s]
            z = lax.dot_general(q[s], kb_ref[pl.ds(start, tk), hs], _NT,
                                preferred_element_type=F32) + bias[s]
            a, c = _sb_core(z, mask, c, uu_ref[...], 1, split=False)
            acc = acc + jnp.dot(a.astype(BF16), vb_ref[pl.ds(start, tk), hs],
                                preferred_element_type=F32)
            out.append((c, acc))
        return tuple(out)

    def group(first, diag, carry):
        for d in reversed(range(nsub)):
            carry = block(first + d, d if diag else None, carry)
        return carry

    zero = (jnp.zeros((tq, 1), F32), jnp.zeros((tq, HEAD_DIM), F32))
    carry = group(qs * nsub, True, (zero,) * hps)
    carry = lax.fori_loop(0, qs, lambda it, ca: group((qs - 1 - it) * nsub, False, ca), carry)
    for s, hs in enumerate(heads):
        o_ref[:, hs] = carry[s][1].astype(o_ref.dtype)


def attention_prompt(q, k, v, bias, *, tq=512, tk=256, hps=8):
    b, t, _ = q.shape
    tq = min(tq, t)
    tk = min(tk, tq)
    assert t % tq == 0 and tq % tk == 0 and N_HEADS % hps == 0
    w = hps * HEAD_DIM
    return pl.pallas_call(
        functools.partial(_attn_prompt_kernel, tq=tq, tk=tk, hps=hps),
        grid=(b, N_HEADS // hps, t // tq),
        in_specs=[pl.BlockSpec(memory_space=pltpu.SMEM),
                  pl.BlockSpec((None, tq, w), lambda bi, h, qi: (bi, qi, h)),
                  pl.BlockSpec((None, t, w), lambda bi, h, qi: (bi, 0, h)),
                  pl.BlockSpec((None, t, w), lambda bi, h, qi: (bi, 0, h))],
        out_specs=pl.BlockSpec((None, tq, w), lambda bi, h, qi: (bi, qi, h)),
        out_shape=jax.ShapeDtypeStruct(q.shape, BF16),
        scratch_shapes=[pltpu.VMEM((tk, tk), BF16)],
        compiler_params=_params("parallel", "parallel", "arbitrary"),
        name="attention_prompt",
    )(bias, q, k, v)


def _attn_sample_kernel(pt_ref, qt_ref, kn_ref, vn_ref, bias_ref, ck_hbm, cv_hbm, o_ref,
                        qbd_ref, acc_ref, c_ref, ll_ref, kb_ref, kbuf, vbuf, sems,
                        *, tq, page, ppstep, n_pages):
    bi = pl.program_id(0)
    j = pl.program_id(1)
    nsteps = pl.num_programs(1)
    cols = N_HEADS * tq
    d = N_HEADS * HEAD_DIM

    def page_copies(step, slot):
        copies = []
        for i in range(ppstep):
            pidx = pt_ref[bi, n_pages - 1 - ((step - 1) * ppstep + i)]
            for t, (hbm, buf) in enumerate(((ck_hbm, kbuf), (cv_hbm, vbuf))):
                copies.append(pltpu.make_async_copy(
                    hbm.at[pidx], buf.at[slot, i, :, pl.ds(0, N_HEADS), :], sems.at[slot, t, i]))
        return copies

    @pl.when(j == 0)
    def _():
        for step in range(1, PAGE_SLOTS):
            @pl.when(step < nsteps)
            def _():
                for cp in page_copies(step, step % PAGE_SLOTS):
                    cp.start()
        rh = lax.broadcasted_iota(jnp.int32, (d, cols), 0) // HEAD_DIM
        ch = lax.broadcasted_iota(jnp.int32, (d, cols), 1) // tq
        qbd_ref[...] = jnp.where(rh == ch, qt_ref[...], jnp.zeros((), BF16))
        acc_ref[...] = jnp.zeros_like(acc_ref)
        c_ref[...] = jnp.zeros_like(c_ref)
        ll_ref[...] = _suffix_ones(page, 0, 2)

    def attend(kb, v_head, n, masked):
        z_all = jnp.dot(kb, qbd_ref[...], preferred_element_type=F32)
        mask = None
        if masked:
            s_idx = lax.broadcasted_iota(jnp.int32, (page, cols), 0)
            t_idx = lax.broadcasted_iota(jnp.int32, (page, cols), 1) % tq
            mask = s_idx < t_idx
        c = c_ref[...]
        p_parts = []
        for i in range(n):
            z = z_all[i * page:(i + 1) * page] + bias_ref[...] * LOG2E
            a, c = _sb_core(z, mask, c, ll_ref[...], 0, split=True)
            p_parts.append(a.T)
        c_ref[...] = c
        p = p_parts[0] if n == 1 else jnp.concatenate(p_parts, axis=1)
        for h in range(N_HEADS):
            rs = slice(h * tq, (h + 1) * tq)
            acc_ref[rs, :] += jnp.dot(p[rs].astype(BF16), v_head(h), preferred_element_type=F32)

    @pl.when(j == 0)
    def _():
        def padded(rows):
            zeros = jnp.zeros((page - tq, rows.shape[1]), BF16)
            return jnp.concatenate([rows.astype(BF16), zeros], axis=0)

        attend(padded(kn_ref[...]),
               lambda h: padded(vn_ref[:, h * HEAD_DIM:(h + 1) * HEAD_DIM]), 1, True)

    @pl.when(j > 0)
    def _():
        slot = j % PAGE_SLOTS
        ahead = j + PAGE_SLOTS - 1

        @pl.when(ahead < nsteps)
        def _():
            for cp in page_copies(ahead, ahead % PAGE_SLOTS):
                cp.start()

        for cp in page_copies(j, slot):
            cp.wait()

        rows_per_page = page * PAGE_PITCH
        kflat = kbuf.reshape(PAGE_SLOTS * ppstep * rows_per_page, HEAD_DIM)
        vflat = vbuf.reshape(PAGE_SLOTS * ppstep * rows_per_page, HEAD_DIM)

        def head_rows(flat, i, h):
            start = (slot * ppstep + i) * rows_per_page + h
            return flat[pl.ds(start, page, stride=PAGE_PITCH), :].astype(BF16)

        for i in range(ppstep):
            for h in range(N_HEADS):
                kb_ref[i * page:(i + 1) * page, h * HEAD_DIM:(h + 1) * HEAD_DIM] = head_rows(kflat, i, h)

        def v_head(h):
            parts = [head_rows(vflat, i, h) for i in range(ppstep)]
            return parts[0] if ppstep == 1 else jnp.concatenate(parts, axis=0)

        attend(kb_ref[...], v_head, ppstep, False)

    @pl.when(j == pl.num_programs(1) - 1)
    def _():
        for h in range(N_HEADS):
            o_ref[:, h * HEAD_DIM:(h + 1) * HEAD_DIM] = acc_ref[h * tq:(h + 1) * tq, :]


def attention_sample(q, k_new, v_new, cache_k, cache_v, page_table, bias, *, ppstep=4):
    db, tq, d = q.shape
    page = cache_k.shape[1]
    n_pages = page_table.shape[1]
    ppstep = min(ppstep, n_pages)
    assert n_pages % ppstep == 0
    cols = N_HEADS * tq
    qt = jnp.tile(jnp.swapaxes(q, 1, 2), (1, 1, N_HEADS))
    bias_row = jnp.repeat(bias, tq).reshape(1, cols)

    def seq_block(bi, j, pt):
        return (bi, 0, 0)

    page_buf = pltpu.VMEM((PAGE_SLOTS, ppstep, page, PAGE_PITCH, HEAD_DIM), F32)
    grid_spec = pltpu.PrefetchScalarGridSpec(
        num_scalar_prefetch=1,
        grid=(db, n_pages // ppstep + 1),
        in_specs=[pl.BlockSpec((None, d, cols), seq_block),
                  pl.BlockSpec((None, tq, d), seq_block),
                  pl.BlockSpec((None, tq, d), seq_block),
                  pl.BlockSpec((1, cols), lambda bi, j, pt: (0, 0)),
                  pl.BlockSpec(memory_space=pl.ANY),
                  pl.BlockSpec(memory_space=pl.ANY)],
        out_specs=pl.BlockSpec((None, tq, d), seq_block),
        scratch_shapes=[pltpu.VMEM((d, cols), BF16),
                        pltpu.VMEM((cols, HEAD_DIM), F32),
                        pltpu.VMEM((1, cols), F32),
                        pltpu.VMEM((page, 2 * page), BF16),
                        pltpu.VMEM((ppstep * page, d), BF16),
                        page_buf, page_buf,
                        pltpu.SemaphoreType.DMA((PAGE_SLOTS, 2, ppstep))],
    )
    return pl.pallas_call(
        functools.partial(_attn_sample_kernel, tq=tq, page=page, ppstep=ppstep, n_pages=n_pages),
        grid_spec=grid_spec,
        out_shape=jax.ShapeDtypeStruct((db, tq, d), F32),
        compiler_params=_params("arbitrary", "arbitrary"),
        name="attention_sample",
    )(page_table, qt, k_new, v_new, bias_row, cache_k, cache_v)


def _trunk(x3s, past, wts):
    dims = [x3.shape for x3 in x3s]
    d = dims[0][2]
    ms = [bb * tt for bb, tt, _ in dims]
    x = [x3.reshape(mm, d) for x3, mm in zip(x3s, ms)]
    n_a = wts["a_w_in"].shape[0]
    depth = wts["w_ffn_up"].shape[0]
    groups = range(2)
    hd = N_HEADS * HEAD_DIM

    def per_group(fn):
        return [list(vals) for vals in zip(*[fn(g) for g in groups])]

    def ffn(x, h, layer):
        hh = matmul_ws(h[0], wts["w_ffn_up"], layer, "relu2", a2=h[1])
        if layer + 1 == depth:
            kw = dict(gains=(wts["norm_final"],), norm_dtype=F32, emit_x=False)
        elif layer + 1 == n_a:
            kw = dict(gains=(wts["norm_mix"][layer + 1], wts["kv_norm"]))
        else:
            kw = dict(gains=(wts["norm_mix"][layer + 1],))
        return per_group(lambda g: matmul_res(hh[g], wts["w_ffn_down"], layer, x[g], **kw))

    def gate(uv, layer, g):
        t, m = dims[g][1], ms[g]
        sub = min(t, CHUNK)
        L = sub if sub == CHUNK else min(m, 64)
        reps = L // sub
        w_tile = jnp.tile(wts["a_w_spatial"][layer][:, :sub, :sub], (1, reps, reps))
        b_rows = jnp.tile(wts["a_b_spatial"][layer][:, :sub].T, (reps, 1))
        b_full = jnp.repeat(b_rows, d // SGU_GROUPS, axis=1)
        return spatial_gate(uv, wts["a_norm_v"][layer], w_tile, b_full, sub=sub, emit_z=g == 1)

    z_states = []
    h = [rmsnorm(x[g], wts["norm_mix"][0], BF16) for g in groups]
    h_kv = None
    for layer in range(n_a):
        uv = matmul_ws(h[0], wts["a_w_in"], layer, "f32", a2=h[1])
        p, z = per_group(lambda g: gate(uv[g], layer, g))
        z_states.append(z[1])
        x, h = per_group(lambda g: matmul_res(p[g], wts["a_w_out"], layer, x[g],
                                              gains=(wts["norm_ffn"][layer],)))
        outs = ffn(x, h, layer)
        x, h = outs[0], outs[1]
        if layer + 1 == n_a:
            h_kv = outs[2]

    k_new = matmul_ws(h_kv[0], wts["w_kv"][None], 0, "f32", a2=h_kv[1], col0=0, n=hd, bf16_copy=True)
    v_new = matmul_ws(h_kv[0], wts["w_kv"][None], 0, "f32", a2=h_kv[1], col0=hd, n=hd, bf16_copy=True)

    q_scale = LOG2E * HEAD_DIM ** -0.5
    cache_k, cache_v, page_table = past
    for layer in range(n_a, depth):
        j = layer - n_a
        q = matmul_ws(h[0], wts["b_w_q"], j, "bf16", a2=h[1], out_scale=q_scale)
        (b0, t0, _), (b1, t1, _) = dims
        o0 = attention_prompt(q[0].reshape(b0, t0, hd), k_new[2].reshape(b0, t0, hd),
                              v_new[2].reshape(b0, t0, hd), wts["b_logit_bias"][j])
        o1 = attention_sample(q[1].reshape(b1, t1, hd), k_new[1].reshape(b1, t1, hd),
                              v_new[1].reshape(b1, t1, hd), cache_k, cache_v, page_table,
                              wts["b_logit_bias"][j])
        o = [o0.reshape(ms[0], hd), o1.reshape(ms[1], hd).astype(BF16)]
        x, h = per_group(lambda g: matmul_res(o[g], wts["b_w_out"], j, x[g],
                                              gains=(wts["norm_ffn"][layer],)))
        outs = ffn(x, h, layer)
        if layer + 1 < depth:
            x, h = outs[0], outs[1]

    y = [outs[0][g].reshape(dims[g]) for g in groups]
    k_out = [k_new[g].reshape(dims[g][0], dims[g][1], N_HEADS, HEAD_DIM) for g in groups]
    v_out = [v_new[g].reshape(dims[g][0], dims[g][1], N_HEADS, HEAD_DIM) for g in groups]
    return y, k_out, v_out, z_states


def kernel(x_prompt, x_sample, cache_k, cache_v, page_table, norm_mix, norm_ffn,
           w_ffn_up, w_ffn_down, a_w_in, a_norm_v, a_w_spatial, a_b_spatial, a_w_out,
           kv_norm, w_kv, b_w_q, b_logit_bias, b_w_out, norm_final):
    wts = dict(
        norm_mix=norm_mix, norm_ffn=norm_ffn,
        w_ffn_up=w_ffn_up, w_ffn_down=w_ffn_down.astype(BF16),
        a_w_in=a_w_in, a_norm_v=a_norm_v, a_w_spatial=a_w_spatial,
        a_b_spatial=a_b_spatial, a_w_out=a_w_out.astype(BF16), kv_norm=kv_norm,
        w_kv=w_kv, b_w_q=b_w_q, b_logit_bias=b_logit_bias,
        b_w_out=b_w_out.astype(BF16), norm_final=norm_final)

    y, k_out, v_out, z_states = _trunk([x_prompt, x_sample], (cache_k, cache_v, page_table), wts)
    db, ts, d = x_sample.shape
    sgu_v_sample = jnp.stack(z_states).reshape(len(z_states), db, ts, d)
    return (y[0], y[1], k_out[0], v_out[0], k_out[1], v_out[1], sgu_v_sample)
```

```python
import functools

import jax
import jax.numpy as jnp
from jax import lax
from jax.experimental import pallas as pl
from jax.experimental.pallas import tpu as pltpu

N_HEADS = 16
HEAD_DIM = 128
CHUNK = 128
SGU_GROUPS = 8
RMS_EPS = 1e-6
LOG2E = 1.4426950408889634
PAGE_PITCH = 24
PAGE_SLOTS = 4

VMEM_LIMIT_BYTES = 56 * 1024 * 1024

F32 = jnp.float32
BF16 = jnp.bfloat16


def _params(*semantics):
    return pltpu.CompilerParams(dimension_semantics=semantics,
                                vmem_limit_bytes=VMEM_LIMIT_BYTES)


def _rmsnorm_kernel(x_ref, g_ref, o_ref):
    x = x_ref[...]
    ms = jnp.mean(x * x, axis=-1, keepdims=True)
    o_ref[...] = (x * lax.rsqrt(ms + RMS_EPS) * g_ref[...]).astype(o_ref.dtype)


def rmsnorm(x, g, out_dtype):
    m, d = x.shape
    tm = min(m, 512)
    assert m % tm == 0
    return pl.pallas_call(
        _rmsnorm_kernel,
        grid=(m // tm,),
        in_specs=[pl.BlockSpec((tm, d), lambda i: (i, 0)),
                  pl.BlockSpec((1, d), lambda i: (0, 0))],
        out_specs=pl.BlockSpec((tm, d), lambda i: (i, 0)),
        out_shape=jax.ShapeDtypeStruct((m, d), out_dtype),
        compiler_params=_params("parallel"),
        name="rmsnorm",
    )(x, g.reshape(1, d))


def _mm_ws_kernel(*refs, epilogue, out_scale, joint, bf16_copy):
    refs = list(refs)
    a_ref = refs.pop(0)
    a2_ref = refs.pop(0) if joint else None
    w_ref = refs.pop(0)
    o_ref = refs.pop(0)
    o2_ref = refs.pop(0) if joint else None
    ob_ref = refs.pop(0) if bf16_copy else None
    (wb_ref,) = refs
    i = pl.program_id(1)

    @pl.when(i == 0)
    def _():
        wb_ref[...] = w_ref[...].astype(BF16)

    def compute(a):
        acc = jnp.dot(a, wb_ref[...], preferred_element_type=F32)
        if epilogue == "relu2":
            h = jnp.maximum(acc, 0.0)
            acc = h * h
        if out_scale is not None:
            acc = acc * out_scale
        return acc

    def main():
        acc = compute(a_ref[...])
        o_ref[...] = acc.astype(o_ref.dtype)
        if bf16_copy:
            ob_ref[...] = acc.astype(BF16)

    if not joint:
        main()
    else:
        @pl.when(i == 0)
        def _():
            o2_ref[...] = compute(a2_ref[...]).astype(o2_ref.dtype)

        pl.when(i > 0)(main)


def matmul_ws(a, w, layer, epilogue, *, a2=None, col0=0, n=None, out_scale=None, bf16_copy=False):
    m, kdim = a.shape
    n = w.shape[2] if n is None else n
    tm = min(m, 2048 if epilogue == "relu2" else 1024)
    tn = min(n, 1024)
    assert m % tm == 0 and n % tn == 0 and col0 % tn == 0 and w.shape[1] == kdim
    jb = col0 // tn
    nm = m // tm
    joint = a2 is not None
    out_dtype = F32 if epilogue == "f32" else BF16
    first = int(joint)
    main_spec = pl.BlockSpec((tm, tn), lambda j, i: (jnp.maximum(i - first, 0), j))
    in_specs = [pl.BlockSpec((tm, kdim), lambda j, i: (jnp.maximum(i - first, 0), 0))]
    out_specs = [main_spec]
    out_shape = [jax.ShapeDtypeStruct((m, n), out_dtype)]
    args = [a]
    if joint:
        m2 = a2.shape[0]
        in_specs.append(pl.BlockSpec((m2, kdim), lambda j, i: (0, 0)))
        out_specs.append(pl.BlockSpec((m2, tn), lambda j, i: (0, j)))
        out_shape.append(jax.ShapeDtypeStruct((m2, n), out_dtype))
        args.append(a2)
    if bf16_copy:
        out_specs.append(main_spec)
        out_shape.append(jax.ShapeDtypeStruct((m, n), BF16))
    in_specs.append(pl.BlockSpec((None, kdim, tn), lambda j, i: (layer, 0, jb + j)))
    args.append(w)
    outs = pl.pallas_call(
        functools.partial(_mm_ws_kernel, epilogue=epilogue, out_scale=out_scale, joint=joint,
                          bf16_copy=bf16_copy),
        grid=(n // tn, nm + first),
        in_specs=in_specs,
        out_specs=out_specs,
        out_shape=out_shape,
        scratch_shapes=[pltpu.VMEM((kdim, tn), BF16)],
        compiler_params=_params("parallel", "arbitrary"),
        name="matmul_ws_" + epilogue,
    )(*args)
    return tuple(outs) if len(outs) > 1 else outs[0]


def _mm_res_kernel(*refs, nk, emit_x, n_norm):
    refs = list(refs)
    a_ref, w_ref, r_ref = refs[:3]
    del refs[:3]
    g_refs = [refs.pop(0) for _ in range(n_norm)]
    x_ref = refs.pop(0) if emit_x else None
    hn_refs = [refs.pop(0) for _ in range(n_norm)]
    scratch = refs

    def finish(acc):
        x = r_ref[...] + acc
        if emit_x:
            x_ref[...] = x
        if n_norm:
            y = x * lax.rsqrt(jnp.mean(x * x, axis=-1, keepdims=True) + RMS_EPS)
            for g_ref, hn_ref in zip(g_refs, hn_refs):
                hn_ref[...] = (y * g_ref[...]).astype(hn_ref.dtype)

    def part():
        return jnp.dot(a_ref[...], w_ref[...], preferred_element_type=F32)

    if nk == 1:
        finish(part())
    else:
        acc_ref = scratch[0]
        k = pl.program_id(1)

        @pl.when(k == 0)
        def _():
            acc_ref[...] = part()

        @pl.when(jnp.logical_and(k > 0, k < nk - 1))
        def _():
            acc_ref[...] += part()

        @pl.when(k == nk - 1)
        def _():
            finish(acc_ref[...] + part())


def matmul_res(a, w, layer, res, *, gains=(), norm_dtype=BF16, emit_x=True):
    m, kdim = a.shape
    d = w.shape[2]
    tm = min(m, 512)
    tk = min(kdim, 2048)
    assert m % tm == 0 and kdim % tk == 0 and res.shape == (m, d)
    nk = kdim // tk
    assert emit_x or gains

    def k_block(i, k):
        return jnp.where(i % 2 == 0, k, nk - 1 - k)

    row_spec = pl.BlockSpec((tm, d), lambda i, k: (i, 0))
    in_specs = [pl.BlockSpec((tm, tk), lambda i, k: (i, k_block(i, k))),
                pl.BlockSpec((None, tk, d), lambda i, k: (layer, k_block(i, k), 0)),
                row_spec]
    args = [a, w, res]
    out_shape, out_specs = [], []
    for g in gains:
        in_specs.append(pl.BlockSpec((1, d), lambda i, k: (0, 0)))
        args.append(g.reshape(1, d))
    if emit_x:
        out_shape.append(jax.ShapeDtypeStruct((m, d), F32))
        out_specs.append(row_spec)
    for _ in gains:
        out_shape.append(jax.ShapeDtypeStruct((m, d), norm_dtype))
        out_specs.append(row_spec)
    outs = pl.pallas_call(
        functools.partial(_mm_res_kernel, nk=nk, emit_x=emit_x, n_norm=len(gains)),
        grid=(m // tm, nk),
        in_specs=in_specs,
        out_specs=out_specs,
        out_shape=out_shape,
        scratch_shapes=[pltpu.VMEM((tm, d), F32)] if nk > 1 else [],
        compiler_params=_params("parallel", "arbitrary"),
        name="matmul_res",
    )(*args)
    return list(outs)


def _gate_kernel(u_ref, v_ref, g_ref, w_ref, b_ref, p_ref, *z_refs, L, sub, nchunk):
    v = v_ref[...]
    ms = jnp.mean(v * v, axis=-1, keepdims=True)
    z = v * lax.rsqrt(ms + RMS_EPS) * g_ref[...]
    if z_refs:
        z_refs[0][...] = z
    zb = z.astype(BF16)
    gdim = v.shape[1] // SGU_GROUPS
    row = lax.broadcasted_iota(jnp.int32, (L, L), 0)
    col = lax.broadcasted_iota(jnp.int32, (L, L), 1)
    mask = jnp.logical_and(row // sub == col // sub, col % sub <= row % sub)
    for g in range(SGU_GROUPS):
        wg = jnp.where(mask, w_ref[g], 0.0).astype(BF16)
        cs = slice(g * gdim, (g + 1) * gdim)
        for c in range(nchunk):
            rs = slice(c * L, (c + 1) * L)
            s = jnp.dot(wg, zb[rs, cs], preferred_element_type=F32) + b_ref[:, cs]
            p_ref[rs, cs] = (u_ref[rs, cs] * s).astype(BF16)


def spatial_gate(uv, g_v, w_tile, b_full, *, sub, emit_z):
    m, e2 = uv.shape
    e = e2 // 2
    L = w_tile.shape[1]
    rows = min(m, 512)
    assert m % rows == 0 and rows % L == 0
    nchunk = rows // L
    out_shape = [jax.ShapeDtypeStruct((m, e), BF16)]
    out_specs = [pl.BlockSpec((rows, e), lambda i: (i, 0))]
    if emit_z:
        out_shape.append(jax.ShapeDtypeStruct((m, e), F32))
        out_specs.append(pl.BlockSpec((rows, e), lambda i: (i, 0)))
    outs = pl.pallas_call(
        functools.partial(_gate_kernel, L=L, sub=sub, nchunk=nchunk),
        grid=(m // rows,),
        in_specs=[pl.BlockSpec((rows, e), lambda i: (i, 0)),
                  pl.BlockSpec((rows, e), lambda i: (i, 1)),
                  pl.BlockSpec((1, e), lambda i: (0, 0)),
                  pl.BlockSpec(w_tile.shape, lambda i: (0, 0, 0)),
                  pl.BlockSpec((L, e), lambda i: (0, 0))],
        out_specs=out_specs,
        out_shape=out_shape,
        compiler_params=_params("parallel"),
        name="spatial_gate",
    )(uv, uv, g_v.reshape(1, e), w_tile, b_full)
    return (outs[0], outs[1]) if emit_z else (outs[0], None)


def _sb_core(z, mask, c, tri, key_axis, split):
    lr = -(jnp.maximum(z, 0.0) + jnp.log2(1.0 + jnp.exp2(-jnp.abs(z))))
    if mask is not None:
        lr = jnp.where(mask, lr, 0.0)
    lrb = lr.astype(BF16)
    if split:
        lrb = jnp.concatenate([lrb, (lr - lrb.astype(F32)).astype(BF16)], axis=key_axis)
    if key_axis == 1:
        incl = jnp.dot(lrb, tri, preferred_element_type=F32)
        total = incl[:, 0:1]
    else:
        incl = jnp.dot(tri, lrb, preferred_element_type=F32)
        total = incl[0:1, :]
    a = jnp.exp2(z + incl + c)
    if mask is not None:
        a = jnp.where(mask, a, 0.0)
    return a, c + total


def _suffix_ones(tk, key_axis, copies):
    shape = (copies * tk, tk) if key_axis == 1 else (tk, copies * tk)
    j = lax.broadcasted_iota(jnp.int32, shape, 1 - key_axis) % tk
    s = lax.broadcasted_iota(jnp.int32, shape, key_axis)
    return (j >= s).astype(BF16)


_NT = (((1,), (1,)), ((), ()))


def _attn_prompt_kernel(bias_ref, q_ref, kb_ref, vb_ref, o_ref, uu_ref, *, tq, tk, hps):
    hp = pl.program_id(1)
    qs = pl.program_id(2)
    nsub = tq // tk
    heads = [slice(s * HEAD_DIM, (s + 1) * HEAD_DIM) for s in range(hps)]

    @pl.when(qs == 0)
    def _():
        uu_ref[...] = _suffix_ones(tk, 1, 1)

    q = [q_ref[:, hs] for hs in heads]
    bias = [bias_ref[hp * hps + s] * LOG2E for s in range(hps)]
    row = lax.broadcasted_iota(jnp.int32, (tq, tk), 0)
    col = lax.broadcasted_iota(jnp.int32, (tq, tk), 1)

    def block(kj, diag, carry):
        start = pl.multiple_of(kj * tk, tk)
        mask = None if diag is None else (col + diag * tk < row)
        out = []
        for s, hs in enumerate(heads):
            c, acc = carry[s]
            z = lax.dot_general(q[s], kb_ref[pl.ds(start, tk), hs], _NT,
                                preferred_element_type=F32) + bias[s]
            a, c = _sb_core(z, mask, c, uu_ref[...], 1, split=False)
            acc = acc + jnp.dot(a.astype(BF16), vb_ref[pl.ds(start, tk), hs],
                                preferred_element_type=F32)
            out.append((c, acc))
        return tuple(out)

    def group(first, diag, carry):
        for d in reversed(range(nsub)):
            carry = block(first + d, d if diag else None, carry)
        return carry

    zero = (jnp.zeros((tq, 1), F32), jnp.zeros((tq, HEAD_DIM), F32))
    carry = group(qs * nsub, True, (zero,) * hps)
    carry = lax.fori_loop(0, qs, lambda it, ca: group((qs - 1 - it) * nsub, False, ca), carry)
    for s, hs in enumerate(heads):
        o_ref[:, hs] = carry[s][1].astype(o_ref.dtype)


def attention_prompt(q, k, v, bias, *, tq=512, tk=256, hps=8):
    b, t, _ = q.shape
    tq = min(tq, t)
    tk = min(tk, tq)
    assert t % tq == 0 and tq % tk == 0 and N_HEADS % hps == 0
    w = hps * HEAD_DIM
    return pl.pallas_call(
        functools.partial(_attn_prompt_kernel, tq=tq, tk=tk, hps=hps),
        grid=(b, N_HEADS // hps, t // tq),
        in_specs=[pl.BlockSpec(memory_space=pltpu.SMEM),
                  pl.BlockSpec((None, tq, w), lambda bi, h, qi: (bi, qi, h)),
                  pl.BlockSpec((None, t, w), lambda bi, h, qi: (bi, 0, h)),
                  pl.BlockSpec((None, t, w), lambda bi, h, qi: (bi, 0, h))],
        out_specs=pl.BlockSpec((None, tq, w), lambda bi, h, qi: (bi, qi, h)),
        out_shape=jax.ShapeDtypeStruct(q.shape, BF16),
        scratch_shapes=[pltpu.VMEM((tk, tk), BF16)],
        compiler_params=_params("parallel", "parallel", "arbitrary"),
        name="attention_prompt",
    )(bias, q, k, v)


def _attn_sample_kernel(pt_ref, qt_ref, kn_ref, vn_ref, bias_ref, ck_hbm, cv_hbm, o_ref,
                        qbd_ref, acc_ref, c_ref, ll_ref, kb_ref, kbuf, vbuf, sems,
                        *, tq, page, ppstep, n_pages):
    bi = pl.program_id(0)
    j = pl.program_id(1)
    nsteps = pl.num_programs(1)
    cols = N_HEADS * tq
    d = N_HEADS * HEAD_DIM

    def page_copies(step, slot):
        copies = []
        for i in range(ppstep):
            pidx = pt_ref[bi, n_pages - 1 - ((step - 1) * ppstep + i)]
            for t, (hbm, buf) in enumerate(((ck_hbm, kbuf), (cv_hbm, vbuf))):
                copies.append(pltpu.make_async_copy(
                    hbm.at[pidx], buf.at[slot, i, :, pl.ds(0, N_HEADS), :], sems.at[slot, t, i]))
        return copies

    @pl.when(j == 0)
    def _():
        for step in range(1, PAGE_SLOTS):
            @pl.when(step < nsteps)
            def _():
                for cp in page_copies(step, step % PAGE_SLOTS):
                    cp.start()
        rh = lax.broadcasted_iota(jnp.int32, (d, cols), 0) // HEAD_DIM
        ch = lax.broadcasted_iota(jnp.int32, (d, cols), 1) // tq
        qbd_ref[...] = jnp.where(rh == ch, qt_ref[...], jnp.zeros((), BF16))
        acc_ref[...] = jnp.zeros_like(acc_ref)
        c_ref[...] = jnp.zeros_like(c_ref)
        ll_ref[...] = _suffix_ones(page, 0, 2)

    def attend(kb, v_head, n, masked):
        z_all = jnp.dot(kb, qbd_ref[...], preferred_element_type=F32)
        mask = None
        if masked:
            s_idx = lax.broadcasted_iota(jnp.int32, (page, cols), 0)
            t_idx = lax.broadcasted_iota(jnp.int32, (page, cols), 1) % tq
            mask = s_idx < t_idx
        c = c_ref[...]
        p_parts = []
        for i in range(n):
            z = z_all[i * page:(i + 1) * page] + bias_ref[...] * LOG2E
            a, c = _sb_core(z, mask, c, ll_ref[...], 0, split=True)
            p_parts.append(a.T)
        c_ref[...] = c
        p = p_parts[0] if n == 1 else jnp.concatenate(p_parts, axis=1)
        for h in range(N_HEADS):
            rs = slice(h * tq, (h + 1) * tq)
            acc_ref[rs, :] += jnp.dot(p[rs].astype(BF16), v_head(h), preferred_element_type=F32)

    @pl.when(j == 0)
    def _():
        def padded(rows):
            zeros = jnp.zeros((page - tq, rows.shape[1]), BF16)
            return jnp.concatenate([rows.astype(BF16), zeros], axis=0)

        attend(padded(kn_ref[...]),
               lambda h: padded(vn_ref[:, h * HEAD_DIM:(h + 1) * HEAD_DIM]), 1, True)

    @pl.when(j > 0)
    def _():
        slot = j % PAGE_SLOTS
        ahead = j + PAGE_SLOTS - 1

        @pl.when(ahead < nsteps)
        def _():
            for cp in page_copies(ahead, ahead % PAGE_SLOTS):
                cp.start()

        for cp in page_copies(j, slot):
            cp.wait()

        rows_per_page = page * PAGE_PITCH
        kflat = kbuf.reshape(PAGE_SLOTS * ppstep * rows_per_page, HEAD_DIM)
        vflat = vbuf.reshape(PAGE_SLOTS * ppstep * rows_per_page, HEAD_DIM)

        def head_rows(flat, i, h):
            start = (slot * ppstep + i) * rows_per_page + h
            return flat[pl.ds(start, page, stride=PAGE_PITCH), :].astype(BF16)

        for i in range(ppstep):
            for h in range(N_HEADS):
                kb_ref[i * page:(i + 1) * page, h * HEAD_DIM:(h + 1) * HEAD_DIM] = head_rows(kflat, i, h)

        def v_head(h):
            parts = [head_rows(vflat, i, h) for i in range(ppstep)]
            return parts[0] if ppstep == 1 else jnp.concatenate(parts, axis=0)

        attend(kb_ref[...], v_head, ppstep, False)

    @pl.when(j == pl.num_programs(1) - 1)
    def _():
        for h in range(N_HEADS):
            o_ref[:, h * HEAD_DIM:(h + 1) * HEAD_DIM] = acc_ref[h * tq:(h + 1) * tq, :]


def attention_sample(q, k_new, v_new, cache_k, cache_v, page_table, bias, *, ppstep=4):
    db, tq, d = q.shape
    page = cache_k.shape[1]
    n_pages = page_table.shape[1]
    ppstep = min(ppstep, n_pages)
    assert n_pages % ppstep == 0
    cols = N_HEADS * tq
    qt = jnp.tile(jnp.swapaxes(q, 1, 2), (1, 1, N_HEADS))
    bias_row = jnp.repeat(bias, tq).reshape(1, cols)

    def seq_block(bi, j, pt):
        return (bi, 0, 0)

    page_buf = pltpu.VMEM((PAGE_SLOTS, ppstep, page, PAGE_PITCH, HEAD_DIM), F32)
    grid_spec = pltpu.PrefetchScalarGridSpec(
        num_scalar_prefetch=1,
        grid=(db, n_pages // ppstep + 1),
        in_specs=[pl.BlockSpec((None, d, cols), seq_block),
                  pl.BlockSpec((None, tq, d), seq_block),
                  pl.BlockSpec((None, tq, d), seq_block),
                  pl.BlockSpec((1, cols), lambda bi, j, pt: (0, 0)),
                  pl.BlockSpec(memory_space=pl.ANY),
                  pl.BlockSpec(memory_space=pl.ANY)],
        out_specs=pl.BlockSpec((None, tq, d), seq_block),
        scratch_shapes=[pltpu.VMEM((d, cols), BF16),
                        pltpu.VMEM((cols, HEAD_DIM), F32),
                        pltpu.VMEM((1, cols), F32),
                        pltpu.VMEM((page, 2 * page), BF16),
                        pltpu.VMEM((ppstep * page, d), BF16),
                        page_buf, page_buf,
                        pltpu.SemaphoreType.DMA((PAGE_SLOTS, 2, ppstep))],
    )
    return pl.pallas_call(
        functools.partial(_attn_sample_kernel, tq=tq, page=page, ppstep=ppstep, n_pages=n_pages),
        grid_spec=grid_spec,
        out_shape=jax.ShapeDtypeStruct((db, tq, d), F32),
        compiler_params=_params("arbitrary", "arbitrary"),
        name="attention_sample",
    )(page_table, qt, k_new, v_new, bias_row, cache_k, cache_v)


def _trunk(x3s, past, wts):
    dims = [x3.shape for x3 in x3s]
    d = dims[0][2]
    ms = [bb * tt for bb, tt, _ in dims]
    x = [x3.reshape(mm, d) for x3, mm in zip(x3s, ms)]
    n_a = wts["a_w_in"].shape[0]
    depth = wts["w_ffn_up"].shape[0]
    groups = range(2)
    hd = N_HEADS * HEAD_DIM

    def per_group(fn):
        return [list(vals) for vals in zip(*[fn(g) for g in groups])]

    def ffn(x, h, layer):
        hh = matmul_ws(h[0], wts["w_ffn_up"], layer, "relu2", a2=h[1])
        if layer + 1 == depth:
            kw = dict(gains=(wts["norm_final"],), norm_dtype=F32, emit_x=False)
        elif layer + 1 == n_a:
            kw = dict(gains=(wts["norm_mix"][layer + 1], wts["kv_norm"]))
        else:
            kw = dict(gains=(wts["norm_mix"][layer + 1],))
        return per_group(lambda g: matmul_res(hh[g], wts["w_ffn_down"], layer, x[g], **kw))

    def gate(uv, layer, g):
        t, m = dims[g][1], ms[g]
        sub = min(t, CHUNK)
        L = sub if sub == CHUNK else min(m, 64)
        reps = L // sub
        w_tile = jnp.tile(wts["a_w_spatial"][layer][:, :sub, :sub], (1, reps, reps))
        b_rows = jnp.tile(wts["a_b_spatial"][layer][:, :sub].T, (reps, 1))
        b_full = jnp.repeat(b_rows, d // SGU_GROUPS, axis=1)
        return spatial_gate(uv, wts["a_norm_v"][layer], w_tile, b_full, sub=sub, emit_z=g == 1)

    z_states = []
    h = [rmsnorm(x[g], wts["norm_mix"][0], BF16) for g in groups]
    h_kv = None
    for layer in range(n_a):
        uv = matmul_ws(h[0], wts["a_w_in"], layer, "f32", a2=h[1])
        p, z = per_group(lambda g: gate(uv[g], layer, g))
        z_states.append(z[1])
        x, h = per_group(lambda g: matmul_res(p[g], wts["a_w_out"], layer, x[g],
                                              gains=(wts["norm_ffn"][layer],)))
        outs = ffn(x, h, layer)
        x, h = outs[0], outs[1]
        if layer + 1 == n_a:
            h_kv = outs[2]

    k_new = matmul_ws(h_kv[0], wts["w_kv"][None], 0, "f32", a2=h_kv[1], col0=0, n=hd, bf16_copy=True)
    v_new = matmul_ws(h_kv[0], wts["w_kv"][None], 0, "f32", a2=h_kv[1], col0=hd, n=hd, bf16_copy=True)

    q_scale = LOG2E * HEAD_DIM ** -0.5
    cache_k, cache_v, page_table = past
    for layer in range(n_a, depth):
        j = layer - n_a
        q = matmul_ws(h[0], wts["b_w_q"], j, "bf16", a2=h[1], out_scale=q_scale)
        (b0, t0, _), (b1, t1, _) = dims
        o0 = attention_prompt(q[0].reshape(b0, t0, hd), k_new[2].reshape(b0, t0, hd),
                              v_new[2].reshape(b0, t0, hd), wts["b_logit_bias"][j])
        o1 = attention_sample(q[1].reshape(b1, t1, hd), k_new[1].reshape(b1, t1, hd),
                              v_new[1].reshape(b1, t1, hd), cache_k, cache_v, page_table,
                              wts["b_logit_bias"][j])
        o = [o0.reshape(ms[0], hd), o1.reshape(ms[1], hd).astype(BF16)]
        x, h = per_group(lambda g: matmul_res(o[g], wts["b_w_out"], j, x[g],
                                              gains=(wts["norm_ffn"][layer],)))
        outs = ffn(x, h, layer)
        if layer + 1 < depth:
            x, h = outs[0], outs[1]

    y = [outs[0][g].reshape(dims[g]) for g in groups]
    k_out = [k_new[g].reshape(dims[g][0], dims[g][1], N_HEADS, HEAD_DIM) for g in groups]
    v_out = [v_new[g].reshape(dims[g][0], dims[g][1], N_HEADS, HEAD_DIM) for g in groups]
    return y, k_out, v_out, z_states


def kernel(x_prompt, x_sample, cache_k, cache_v, page_table, norm_mix, norm_ffn,
           w_ffn_up, w_ffn_down, a_w_in, a_norm_v, a_w_spatial, a_b_spatial, a_w_out,
           kv_norm, w_kv, b_w_q, b_logit_bias, b_w_out, norm_final):
    wts = dict(
        norm_mix=norm_mix, norm_ffn=norm_ffn,
        w_ffn_up=w_ffn_up, w_ffn_down=w_ffn_down.astype(BF16),
        a_w_in=a_w_in, a_norm_v=a_norm_v, a_w_spatial=a_w_spatial,
        a_b_spatial=a_b_spatial, a_w_out=a_w_out.astype(BF16), kv_norm=kv_norm,
        w_kv=w_kv, b_w_q=b_w_q, b_logit_bias=b_logit_bias,
        b_w_out=b_w_out.astype(BF16), norm_final=norm_final)

    y, k_out, v_out, z_states = _trunk([x_prompt, x_sample], (cache_k, cache_v, page_table), wts)
    db, ts, d = x_sample.shape
    sgu_v_sample = jnp.stack(z_states).reshape(len(z_states), db, ts, d)
    return (y[0], y[1], k_out[0], v_out[0], k_out[1], v_out[1], sgu_v_sample)
```

```python
import functools

import jax
import jax.numpy as jnp
from jax import lax
from jax.experimental import pallas as pl
from jax.experimental.pallas import tpu as pltpu

N_HEADS = 16
HEAD_DIM = 128
CHUNK = 128
SGU_GROUPS = 8
RMS_EPS = 1e-6
LOG2E = 1.4426950408889634
PAGE_PITCH = 24
PAGE_SLOTS = 4

VMEM_LIMIT_BYTES = 56 * 1024 * 1024

F32 = jnp.float32
BF16 = jnp.bfloat16


def _params(*semantics):
    return pltpu.CompilerParams(dimension_semantics=semantics,
                                vmem_limit_bytes=VMEM_LIMIT_BYTES)


def _rmsnorm_kernel(x_ref, g_ref, o_ref):
    x = x_ref[...]
    ms = jnp.mean(x * x, axis=-1, keepdims=True)
    o_ref[...] = (x * lax.rsqrt(ms + RMS_EPS) * g_ref[...]).astype(o_ref.dtype)


def rmsnorm(x, g, out_dtype):
    m, d = x.shape
    tm = min(m, 512)
    assert m % tm == 0
    return pl.pallas_call(
        _rmsnorm_kernel,
        grid=(m // tm,),
        in_specs=[pl.BlockSpec((tm, d), lambda i: (i, 0)),
                  pl.BlockSpec((1, d), lambda i: (0, 0))],
        out_specs=pl.BlockSpec((tm, d), lambda i: (i, 0)),
        out_shape=jax.ShapeDtypeStruct((m, d), out_dtype),
        compiler_params=_params("parallel"),
        name="rmsnorm",
    )(x, g.reshape(1, d))


def _mm_ws_kernel(*refs, epilogue, out_scale, joint, bf16_copy):
    refs = list(refs)
    a_ref = refs.pop(0)
    a2_ref = refs.pop(0) if joint else None
    w_ref = refs.pop(0)
    o_ref = refs.pop(0)
    o2_ref = refs.pop(0) if joint else None
    ob_ref = refs.pop(0) if bf16_copy else None
    (wb_ref,) = refs
    i = pl.program_id(1)

    @pl.when(i == 0)
    def _():
        wb_ref[...] = w_ref[...].astype(BF16)

    def compute(a):
        acc = jnp.dot(a, wb_ref[...], preferred_element_type=F32)
        if epilogue == "relu2":
            h = jnp.maximum(acc, 0.0)
            acc = h * h
        if out_scale is not None:
            acc = acc * out_scale
        return acc

    def main():
        acc = compute(a_ref[...])
        o_ref[...] = acc.astype(o_ref.dtype)
        if bf16_copy:
            ob_ref[...] = acc.astype(BF16)

    if not joint:
        main()
    else:
        @pl.when(i == 0)
        def _():
            o2_ref[...] = compute(a2_ref[...]).astype(o2_ref.dtype)

        pl.when(i > 0)(main)


def matmul_ws(a, w, layer, epilogue, *, a2=None, col0=0, n=None, out_scale=None, bf16_copy=False):
    m, kdim = a.shape
    n = w.shape[2] if n is None else n
    tm = min(m, 2048 if epilogue == "relu2" else 1024)
    tn = min(n, 1024)
    assert m % tm == 0 and n % tn == 0 and col0 % tn == 0 and w.shape[1] == kdim
    jb = col0 // tn
    nm = m // tm
    joint = a2 is not None
    out_dtype = F32 if epilogue == "f32" else BF16
    first = int(joint)
    main_spec = pl.BlockSpec((tm, tn), lambda j, i: (jnp.maximum(i - first, 0), j))
    in_specs = [pl.BlockSpec((tm, kdim), lambda j, i: (jnp.maximum(i - first, 0), 0))]
    out_specs = [main_spec]
    out_shape = [jax.ShapeDtypeStruct((m, n), out_dtype)]
    args = [a]
    if joint:
        m2 = a2.shape[0]
        in_specs.append(pl.BlockSpec((m2, kdim), lambda j, i: (0, 0)))
        out_specs.append(pl.BlockSpec((m2, tn), lambda j, i: (0, j)))
        out_shape.append(jax.ShapeDtypeStruct((m2, n), out_dtype))
        args.append(a2)
    if bf16_copy:
        out_specs.append(main_spec)
        out_shape.append(jax.ShapeDtypeStruct((m, n), BF16))
    in_specs.append(pl.BlockSpec((None, kdim, tn), lambda j, i: (layer, 0, jb + j)))
    args.append(w)
    outs = pl.pallas_call(
        functools.partial(_mm_ws_kernel, epilogue=epilogue, out_scale=out_scale, joint=joint,
                          bf16_copy=bf16_copy),
        grid=(n // tn, nm + first),
        in_specs=in_specs,
        out_specs=out_specs,
        out_shape=out_shape,
        scratch_shapes=[pltpu.VMEM((kdim, tn), BF16)],
        compiler_params=_params("parallel", "arbitrary"),
        name="matmul_ws_" + epilogue,
    )(*args)
    return tuple(outs) if len(outs) > 1 else outs[0]


def _mm_res_kernel(*refs, nk, emit_x, n_norm):
    refs = list(refs)
    a_ref, w_ref, r_ref = refs[:3]
    del refs[:3]
    g_refs = [refs.pop(0) for _ in range(n_norm)]
    x_ref = refs.pop(0) if emit_x else None
    hn_refs = [refs.pop(0) for _ in range(n_norm)]
    scratch = refs

    def finish(acc):
        x = r_ref[...] + acc
        if emit_x:
            x_ref[...] = x
        if n_norm:
            y = x * lax.rsqrt(jnp.mean(x * x, axis=-1, keepdims=True) + RMS_EPS)
            for g_ref, hn_ref in zip(g_refs, hn_refs):
                hn_ref[...] = (y * g_ref[...]).astype(hn_ref.dtype)

    def part():
        return jnp.dot(a_ref[...], w_ref[...], preferred_element_type=F32)

    if nk == 1:
        finish(part())
    else:
        acc_ref = scratch[0]
        k = pl.program_id(1)

        @pl.when(k == 0)
        def _():
            acc_ref[...] = part()

        @pl.when(jnp.logical_and(k > 0, k < nk - 1))
        def _():
            acc_ref[...] += part()

        @pl.when(k == nk - 1)
        def _():
            finish(acc_ref[...] + part())


def matmul_res(a, w, layer, res, *, gains=(), norm_dtype=BF16, emit_x=True):
    m, kdim = a.shape
    d = w.shape[2]
    tm = min(m, 512)
    tk = min(kdim, 2048)
    assert m % tm == 0 and kdim % tk == 0 and res.shape == (m, d)
    nk = kdim // tk
    assert emit_x or gains

    def k_block(i, k):
        return jnp.where(i % 2 == 0, k, nk - 1 - k)

    row_spec = pl.BlockSpec((tm, d), lambda i, k: (i, 0))
    in_specs = [pl.BlockSpec((tm, tk), lambda i, k: (i, k_block(i, k))),
                pl.BlockSpec((None, tk, d), lambda i, k: (layer, k_block(i, k), 0)),
                row_spec]
    args = [a, w, res]
    out_shape, out_specs = [], []
    for g in gains:
        in_specs.append(pl.BlockSpec((1, d), lambda i, k: (0, 0)))
        args.append(g.reshape(1, d))
    if emit_x:
        out_shape.append(jax.ShapeDtypeStruct((m, d), F32))
        out_specs.append(row_spec)
    for _ in gains:
        out_shape.append(jax.ShapeDtypeStruct((m, d), norm_dtype))
        out_specs.append(row_spec)
    outs = pl.pallas_call(
        functools.partial(_mm_res_kernel, nk=nk, emit_x=emit_x, n_norm=len(gains)),
        grid=(m // tm, nk),
        in_specs=in_specs,
        out_specs=out_specs,
        out_shape=out_shape,
        scratch_shapes=[pltpu.VMEM((tm, d), F32)] if nk > 1 else [],
        compiler_params=_params("parallel", "arbitrary"),
        name="matmul_res",
    )(*args)
    return list(outs)


def _gate_kernel(u_ref, v_ref, g_ref, w_ref, b_ref, p_ref, *z_refs, L, sub, nchunk):
    v = v_ref[...]
    ms = jnp.mean(v * v, axis=-1, keepdims=True)
    z = v * lax.rsqrt(ms + RMS_EPS) * g_ref[...]
    if z_refs:
        z_refs[0][...] = z
    zb = z.astype(BF16)
    gdim = v.shape[1] // SGU_GROUPS
    row = lax.broadcasted_iota(jnp.int32, (L, L), 0)
    col = lax.broadcasted_iota(jnp.int32, (L, L), 1)
    mask = jnp.logical_and(row // sub == col // sub, col % sub <= row % sub)
    for g in range(SGU_GROUPS):
        wg = jnp.where(mask, w_ref[g], 0.0).astype(BF16)
        cs = slice(g * gdim, (g + 1) * gdim)
        for c in range(nchunk):
            rs = slice(c * L, (c + 1) * L)
            s = jnp.dot(wg, zb[rs, cs], preferred_element_type=F32) + b_ref[:, cs]
            p_ref[rs, cs] = (u_ref[rs, cs] * s).astype(BF16)


def spatial_gate(uv, g_v, w_tile, b_full, *, sub, emit_z):
    m, e2 = uv.shape
    e = e2 // 2
    L = w_tile.shape[1]
    rows = min(m, 512)
    assert m % rows == 0 and rows % L == 0
    nchunk = rows // L
    out_shape = [jax.ShapeDtypeStruct((m, e), BF16)]
    out_specs = [pl.BlockSpec((rows, e), lambda i: (i, 0))]
    if emit_z:
        out_shape.append(jax.ShapeDtypeStruct((m, e), F32))
        out_specs.append(pl.BlockSpec((rows, e), lambda i: (i, 0)))
    outs = pl.pallas_call(
        functools.partial(_gate_kernel, L=L, sub=sub, nchunk=nchunk),
        grid=(m // rows,),
        in_specs=[pl.BlockSpec((rows, e), lambda i: (i, 0)),
                  pl.BlockSpec((rows, e), lambda i: (i, 1)),
                  pl.BlockSpec((1, e), lambda i: (0, 0)),
                  pl.BlockSpec(w_tile.shape, lambda i: (0, 0, 0)),
                  pl.BlockSpec((L, e), lambda i: (0, 0))],
        out_specs=out_specs,
        out_shape=out_shape,
        compiler_params=_params("parallel"),
        name="spatial_gate",
    )(uv, uv, g_v.reshape(1, e), w_tile, b_full)
    return (outs[0], outs[1]) if emit_z else (outs[0], None)


def _sb_core(z, mask, c, tri, key_axis, split):
    lr = -(jnp.maximum(z, 0.0) + jnp.log2(1.0 + jnp.exp2(-jnp.abs(z))))
    if mask is not None:
        lr = jnp.where(mask, lr, 0.0)
    lrb = lr.astype(BF16)
    if split:
        lrb = jnp.concatenate([lrb, (lr - lrb.astype(F32)).astype(BF16)], axis=key_axis)
    if key_axis == 1:
        incl = jnp.dot(lrb, tri, preferred_element_type=F32)
        total = incl[:, 0:1]
    else:
        incl = jnp.dot(tri, lrb, preferred_element_type=F32)
        total = incl[0:1, :]
    a = jnp.exp2(z + incl + c)
    if mask is not None:
        a = jnp.where(mask, a, 0.0)
    return a, c + total


def _suffix_ones(tk, key_axis, copies):
    shape = (copies * tk, tk) if key_axis == 1 else (tk, copies * tk)
    j = lax.broadcasted_iota(jnp.int32, shape, 1 - key_axis) % tk
    s = lax.broadcasted_iota(jnp.int32, shape, key_axis)
    return (j >= s).astype(BF16)


_NT = (((1,), (1,)), ((), ()))


def _attn_prompt_kernel(bias_ref, q_ref, kb_ref, vb_ref, o_ref, uu_ref, *, tq, tk, hps):
    hp = pl.program_id(1)
    qs = pl.program_id(2)
    nsub = tq // tk
    heads = [slice(s * HEAD_DIM, (s + 1) * HEAD_DIM) for s in range(hps)]

    @pl.when(qs == 0)
    def _():
        uu_ref[...] = _suffix_ones(tk, 1, 1)

    q = [q_ref[:, hs] for hs in heads]
    bias = [bias_ref[hp * hps + s] * LOG2E for s in range(hps)]
    row = lax.broadcasted_iota(jnp.int32, (tq, tk), 0)
    col = lax.broadcasted_iota(jnp.int32, (tq, tk), 1)

    def block(kj, diag, carry):
        start = pl.multiple_of(kj * tk, tk)
        mask = None if diag is None else (col + diag * tk < row)
        out = []
        for s, hs in enumerate(heads):
            c, acc = carry[s]
            z = lax.dot_general(q[s], kb_ref[pl.ds(start, tk), hs], _NT,
                                preferred_element_type=F32) + bias[s]
            a, c = _sb_core(z, mask, c, uu_ref[...], 1, split=False)
            acc = acc + jnp.dot(a.astype(BF16), vb_ref[pl.ds(start, tk), hs],
                                preferred_element_type=F32)
            out.append((c, acc))
        return tuple(out)

    def group(first, diag, carry):
        for d in reversed(range(nsub)):
            carry = block(first + d, d if diag else None, carry)
        return carry

    zero = (jnp.zeros((tq, 1), F32), jnp.zeros((tq, HEAD_DIM), F32))
    carry = group(qs * nsub, True, (zero,) * hps)
    carry = lax.fori_loop(0, qs, lambda it, ca: group((qs - 1 - it) * nsub, False, ca), carry)
    for s, hs in enumerate(heads):
        o_ref[:, hs] = carry[s][1].astype(o_ref.dtype)


def attention_prompt(q, k, v, bias, *, tq=512, tk=256, hps=8):
    b, t, _ = q.shape
    tq = min(tq, t)
    tk = min(tk, tq)
    assert t % tq == 0 and tq % tk == 0 and N_HEADS % hps == 0
    w = hps * HEAD_DIM
    return pl.pallas_call(
        functools.partial(_attn_prompt_kernel, tq=tq, tk=tk, hps=hps),
        grid=(b, N_HEADS // hps, t // tq),
        in_specs=[pl.BlockSpec(memory_space=pltpu.SMEM),
                  pl.BlockSpec((None, tq, w), lambda bi, h, qi: (bi, qi, h)),
                  pl.BlockSpec((None, t, w), lambda bi, h, qi: (bi, 0, h)),
                  pl.BlockSpec((None, t, w), lambda bi, h, qi: (bi, 0, h))],
        out_specs=pl.BlockSpec((None, tq, w), lambda bi, h, qi: (bi, qi, h)),
        out_shape=jax.ShapeDtypeStruct(q.shape, BF16),
        scratch_shapes=[pltpu.VMEM((tk, tk), BF16)],
        compiler_params=_params("parallel", "parallel", "arbitrary"),
        name="attention_prompt",
    )(bias, q, k, v)


def _attn_sample_kernel(pt_ref, qt_ref, kn_ref, vn_ref, bias_ref, ck_hbm, cv_hbm, o_ref,
                        qbd_ref, acc_ref, c_ref, ll_ref, kb_ref, kbuf, vbuf, sems,
                        *, tq, page, ppstep, n_pages):
    bi = pl.program_id(0)
    j = pl.program_id(1)
    nsteps = pl.num_programs(1)
    cols = N_HEADS * tq
    d = N_HEADS * HEAD_DIM

    def page_copies(step, slot):
        copies = []
        for i in range(ppstep):
            pidx = pt_ref[bi, n_pages - 1 - ((step - 1) * ppstep + i)]
            for t, (hbm, buf) in enumerate(((ck_hbm, kbuf), (cv_hbm, vbuf))):
                copies.append(pltpu.make_async_copy(
                    hbm.at[pidx], buf.at[slot, i, :, pl.ds(0, N_HEADS), :], sems.at[slot, t, i]))
        return copies

    def start_copies(copies):
        for n, cp in enumerate(copies):
            cp.start(priority=n % 2)

    @pl.when(j == 0)
    def _():
        for step in range(1, PAGE_SLOTS):
            @pl.when(step < nsteps)
            def _():
                start_copies(page_copies(step, step % PAGE_SLOTS))
        rh = lax.broadcasted_iota(jnp.int32, (d, cols), 0) // HEAD_DIM
        ch = lax.broadcasted_iota(jnp.int32, (d, cols), 1) // tq
        qbd_ref[...] = jnp.where(rh == ch, qt_ref[...], jnp.zeros((), BF16))
        acc_ref[...] = jnp.zeros_like(acc_ref)
        c_ref[...] = jnp.zeros_like(c_ref)
        ll_ref[...] = _suffix_ones(page, 0, 2)

    def attend(kb, v_head, n, masked):
        z_all = jnp.dot(kb, qbd_ref[...], preferred_element_type=F32)
        mask = None
        if masked:
            s_idx = lax.broadcasted_iota(jnp.int32, (page, cols), 0)
            t_idx = lax.broadcasted_iota(jnp.int32, (page, cols), 1) % tq
            mask = s_idx < t_idx
        c = c_ref[...]
        p_parts = []
        for i in range(n):
            z = z_all[i * page:(i + 1) * page] + bias_ref[...] * LOG2E
            a, c = _sb_core(z, mask, c, ll_ref[...], 0, split=True)
            p_parts.append(a.T)
        c_ref[...] = c
        p = p_parts[0] if n == 1 else jnp.concatenate(p_parts, axis=1)
        for h in range(N_HEADS):
            rs = slice(h * tq, (h + 1) * tq)
            acc_ref[rs, :] += jnp.dot(p[rs].astype(BF16), v_head(h), preferred_element_type=F32)

    @pl.when(j == 0)
    def _():
        def padded(rows):
            zeros = jnp.zeros((page - tq, rows.shape[1]), BF16)
            return jnp.concatenate([rows.astype(BF16), zeros], axis=0)

        attend(padded(kn_ref[...]),
               lambda h: padded(vn_ref[:, h * HEAD_DIM:(h + 1) * HEAD_DIM]), 1, True)

    @pl.when(j > 0)
    def _():
        slot = j % PAGE_SLOTS
        ahead = j + PAGE_SLOTS - 1

        @pl.when(ahead < nsteps)
        def _():
            start_copies(page_copies(ahead, ahead % PAGE_SLOTS))

        for cp in page_copies(j, slot):
            cp.wait()

        rows_per_page = page * PAGE_PITCH
        kflat = kbuf.reshape(PAGE_SLOTS * ppstep * rows_per_page, HEAD_DIM)
        vflat = vbuf.reshape(PAGE_SLOTS * ppstep * rows_per_page, HEAD_DIM)

        def head_rows(flat, i, h):
            start = (slot * ppstep + i) * rows_per_page + h
            return flat[pl.ds(start, page, stride=PAGE_PITCH), :].astype(BF16)

        for i in range(ppstep):
            for h in range(N_HEADS):
                kb_ref[i * page:(i + 1) * page, h * HEAD_DIM:(h + 1) * HEAD_DIM] = head_rows(kflat, i, h)

        def v_head(h):
            parts = [head_rows(vflat, i, h) for i in range(ppstep)]
            return parts[0] if ppstep == 1 else jnp.concatenate(parts, axis=0)

        attend(kb_ref[...], v_head, ppstep, False)

    @pl.when(j == pl.num_programs(1) - 1)
    def _():
        for h in range(N_HEADS):
            o_ref[:, h * HEAD_DIM:(h + 1) * HEAD_DIM] = acc_ref[h * tq:(h + 1) * tq, :]


def attention_sample(q, k_new, v_new, cache_k, cache_v, page_table, bias, *, ppstep=4):
    db, tq, d = q.shape
    page = cache_k.shape[1]
    n_pages = page_table.shape[1]
    ppstep = min(ppstep, n_pages)
    assert n_pages % ppstep == 0
    cols = N_HEADS * tq
    qt = jnp.tile(jnp.swapaxes(q, 1, 2), (1, 1, N_HEADS))
    bias_row = jnp.repeat(bias, tq).reshape(1, cols)

    def seq_block(bi, j, pt):
        return (bi, 0, 0)

    page_buf = pltpu.VMEM((PAGE_SLOTS, ppstep, page, PAGE_PITCH, HEAD_DIM), F32)
    grid_spec = pltpu.PrefetchScalarGridSpec(
        num_scalar_prefetch=1,
        grid=(db, n_pages // ppstep + 1),
        in_specs=[pl.BlockSpec((None, d, cols), seq_block),
                  pl.BlockSpec((None, tq, d), seq_block),
                  pl.BlockSpec((None, tq, d), seq_block),
                  pl.BlockSpec((1, cols), lambda bi, j, pt: (0, 0)),
                  pl.BlockSpec(memory_space=pl.ANY),
                  pl.BlockSpec(memory_space=pl.ANY)],
        out_specs=pl.BlockSpec((None, tq, d), seq_block),
        scratch_shapes=[pltpu.VMEM((d, cols), BF16),
                        pltpu.VMEM((cols, HEAD_DIM), F32),
                        pltpu.VMEM((1, cols), F32),
                        pltpu.VMEM((page, 2 * page), BF16),
                        pltpu.VMEM((ppstep * page, d), BF16),
                        page_buf, page_buf,
                        pltpu.SemaphoreType.DMA((PAGE_SLOTS, 2, ppstep))],
    )
    return pl.pallas_call(
        functools.partial(_attn_sample_kernel, tq=tq, page=page, ppstep=ppstep, n_pages=n_pages),
        grid_spec=grid_spec,
        out_shape=jax.ShapeDtypeStruct((db, tq, d), F32),
        compiler_params=_params("arbitrary", "arbitrary"),
        name="attention_sample",
    )(page_table, qt, k_new, v_new, bias_row, cache_k, cache_v)


def _trunk(x3s, past, wts):
    dims = [x3.shape for x3 in x3s]
    d = dims[0][2]
    ms = [bb * tt for bb, tt, _ in dims]
    x = [x3.reshape(mm, d) for x3, mm in zip(x3s, ms)]
    n_a = wts["a_w_in"].shape[0]
    depth = wts["w_ffn_up"].shape[0]
    groups = range(2)
    hd = N_HEADS * HEAD_DIM

    def per_group(fn):
        return [list(vals) for vals in zip(*[fn(g) for g in groups])]

    def ffn(x, h, layer):
        hh = matmul_ws(h[0], wts["w_ffn_up"], layer, "relu2", a2=h[1])
        if layer + 1 == depth:
            kw = dict(gains=(wts["norm_final"],), norm_dtype=F32, emit_x=False)
        elif layer + 1 == n_a:
            kw = dict(gains=(wts["norm_mix"][layer + 1], wts["kv_norm"]))
        else:
            kw = dict(gains=(wts["norm_mix"][layer + 1],))
        return per_group(lambda g: matmul_res(hh[g], wts["w_ffn_down"], layer, x[g], **kw))

    def gate(uv, layer, g):
        t, m = dims[g][1], ms[g]
        sub = min(t, CHUNK)
        L = sub if sub == CHUNK else min(m, 64)
        reps = L // sub
        w_tile = jnp.tile(wts["a_w_spatial"][layer][:, :sub, :sub], (1, reps, reps))
        b_rows = jnp.tile(wts["a_b_spatial"][layer][:, :sub].T, (reps, 1))
        b_full = jnp.repeat(b_rows, d // SGU_GROUPS, axis=1)
        return spatial_gate(uv, wts["a_norm_v"][layer], w_tile, b_full, sub=sub, emit_z=g == 1)

    z_states = []
    h = [rmsnorm(x[g], wts["norm_mix"][0], BF16) for g in groups]
    h_kv = None
    for layer in range(n_a):
        uv = matmul_ws(h[0], wts["a_w_in"], layer, "f32", a2=h[1])
        p, z = per_group(lambda g: gate(uv[g], layer, g))
        z_states.append(z[1])
        x, h = per_group(lambda g: matmul_res(p[g], wts["a_w_out"], layer, x[g],
                                              gains=(wts["norm_ffn"][layer],)))
        outs = ffn(x, h, layer)
        x, h = outs[0], outs[1]
        if layer + 1 == n_a:
            h_kv = outs[2]

    k_new = matmul_ws(h_kv[0], wts["w_kv"][None], 0, "f32", a2=h_kv[1], col0=0, n=hd, bf16_copy=True)
    v_new = matmul_ws(h_kv[0], wts["w_kv"][None], 0, "f32", a2=h_kv[1], col0=hd, n=hd, bf16_copy=True)

    q_scale = LOG2E * HEAD_DIM ** -0.5
    cache_k, cache_v, page_table = past
    for layer in range(n_a, depth):
        j = layer - n_a
        q = matmul_ws(h[0], wts["b_w_q"], j, "bf16", a2=h[1], out_scale=q_scale)
        (b0, t0, _), (b1, t1, _) = dims
        o0 = attention_prompt(q[0].reshape(b0, t0, hd), k_new[2].reshape(b0, t0, hd),
                              v_new[2].reshape(b0, t0, hd), wts["b_logit_bias"][j])
        o1 = attention_sample(q[1].reshape(b1, t1, hd), k_new[1].reshape(b1, t1, hd),
                              v_new[1].reshape(b1, t1, hd), cache_k, cache_v, page_table,
                              wts["b_logit_bias"][j])
        o = [o0.reshape(ms[0], hd), o1.reshape(ms[1], hd).astype(BF16)]
        x, h = per_group(lambda g: matmul_res(o[g], wts["b_w_out"], j, x[g],
                                              gains=(wts["norm_ffn"][layer],)))
        outs = ffn(x, h, layer)
        if layer + 1 < depth:
            x, h = outs[0], outs[1]

    y = [outs[0][g].reshape(dims[g]) for g in groups]
    k_out = [k_new[g].reshape(dims[g][0], dims[g][1], N_HEADS, HEAD_DIM) for g in groups]
    v_out = [v_new[g].reshape(dims[g][0], dims[g][1], N_HEADS, HEAD_DIM) for g in groups]
    return y, k_out, v_out, z_states


def kernel(x_prompt, x_sample, cache_k, cache_v, page_table, norm_mix, norm_ffn,
           w_ffn_up, w_ffn_down, a_w_in, a_norm_v, a_w_spatial, a_b_spatial, a_w_out,
           kv_norm, w_kv, b_w_q, b_logit_bias, b_w_out, norm_final):
    wts = dict(
        norm_mix=norm_mix, norm_ffn=norm_ffn,
        w_ffn_up=w_ffn_up, w_ffn_down=w_ffn_down.astype(BF16),
        a_w_in=a_w_in, a_norm_v=a_norm_v, a_w_spatial=a_w_spatial,
        a_b_spatial=a_b_spatial, a_w_out=a_w_out.astype(BF16), kv_norm=kv_norm,
        w_kv=w_kv, b_w_q=b_w_q, b_logit_bias=b_logit_bias,
        b_w_out=b_w_out.astype(BF16), norm_final=norm_final)

    y, k_out, v_out, z_states = _trunk([x_prompt, x_sample], (cache_k, cache_v, page_table), wts)
    db, ts, d = x_sample.shape
    sgu_v_sample = jnp.stack(z_states).reshape(len(z_states), db, ts, d)
    return (y[0], y[1], k_out[0], v_out[0], k_out[1], v_out[1], sgu_v_sample)
```
